```python
import math
import jax, jax.numpy as jnp
from jax import lax
import numpy as np

D_MODEL = 2048
BATCH = 8
SEQ = 2048
DEPTH = 1

N_META = 16
N_HEADS = 8
N_KV_HEADS = 2
HEAD_DIM = 128
ATTN_WIDTH = N_HEADS * HEAD_DIM
KV_WIDTH = N_KV_HEADS * HEAD_DIM
N_IDX_HEADS = 16
IDX_DIM = 64
TOPK_MAX = 256
CONV_WIDTH = D_MODEL // 2
CONV_K = 3
D_FF = 5632
ROPE_THETA = 500000.0
ROT_DIV = 4
Q_BLOCK = 128
EPS = 1e-6

kernel_name = "hybrid_dsa_shortconv_macaron_block"

SPLITS = [ATTN_WIDTH, KV_WIDTH, KV_WIDTH, N_IDX_HEADS * IDX_DIM, IDX_DIM, N_IDX_HEADS,
          CONV_WIDTH, CONV_WIDTH, CONV_WIDTH, D_MODEL, D_MODEL]
IN_COLS = int(sum(SPLITS))
SPLIT_POINTS = [int(v) for v in np.cumsum(SPLITS)[:-1]]


def rms_norm(x, g):
    xf = x.astype(jnp.float32)
    y = xf * lax.rsqrt(jnp.mean(xf * xf, axis=-1, keepdims=True) + EPS)
    return (y * g.astype(jnp.float32)).astype(x.dtype)


def swiglu(x, w_gate, w_up, w_down):
    return (jax.nn.silu(x @ w_gate) * (x @ w_up)) @ w_down


def rope_tables(n_pos, rot_dim):
    inv = ROPE_THETA ** (-jnp.arange(0, rot_dim, 2, dtype=jnp.float32) / rot_dim)
    ang = jnp.arange(n_pos, dtype=jnp.float32)[:, None] * inv[None, :]
    return jnp.cos(ang), jnp.sin(ang)


def partial_rope(x, cos, sin):
    half = cos.shape[-1]
    rot = 2 * half
    x1 = x[..., :half].astype(jnp.float32)
    x2 = x[..., half:rot].astype(jnp.float32)
    c = cos[None, :, None, :]
    s = sin[None, :, None, :]
    r1 = (x1 * c - x2 * s).astype(x.dtype)
    r2 = (x2 * c + x1 * s).astype(x.dtype)
    return jnp.concatenate([r1, r2, x[..., rot:]], axis=-1)


def causal_depthwise_conv(x, w, b):
    y = lax.conv_general_dilated(
        x, w[:, None, :].astype(x.dtype), window_strides=(1,),
        padding=[(CONV_K - 1, 0)], dimension_numbers=('NWC', 'WIO', 'NWC'),
        feature_group_count=x.shape[-1])
    return y + b.astype(x.dtype)


def dsa_sparse_attention(q, k, v, q_idx, k_idx, w_idx, k_sel):
    B, T = q.shape[0], q.shape[1]
    n_blk = -(-T // Q_BLOCK)
    Tp = n_blk * Q_BLOCK
    pad = Tp - T
    rep = N_HEADS // N_KV_HEADS

    def to_blocks(a):
        a = jnp.pad(a, [(0, 0), (0, pad)] + [(0, 0)] * (a.ndim - 2))
        return jnp.moveaxis(a.reshape((B, n_blk, Q_BLOCK) + a.shape[2:]), 1, 0)

    qpos = jnp.arange(Tp, dtype=jnp.int32).reshape(n_blk, Q_BLOCK)
    kpos = jnp.arange(T, dtype=jnp.int32)

    def block(args):
        qb, qib, wb, pb = args
        causal = kpos[None, :] <= pb[:, None]
        dots = jnp.einsum('bqhd,bsd->bqhs', qib, k_idx)
        isc = jnp.einsum('bqhs,bqh->bqs', jax.nn.relu(dots), wb).astype(jnp.float32)
        isc = jnp.where(causal[None], isc, -jnp.inf)
        _, sel = lax.top_k(isc, k_sel)
        valid = sel <= pb[None, :, None]
        ks = jax.vmap(lambda a, i: a[i])(k, sel)
        vs = jax.vmap(lambda a, i: a[i])(v, sel)
        qg = qb.reshape(B, Q_BLOCK, N_KV_HEADS, rep, HEAD_DIM)
        s = jnp.einsum('bqgrd,bqkgd->bqgrk', qg, ks).astype(jnp.float32) * (HEAD_DIM ** -0.5)
        s = jnp.where(valid[:, :, None, None, :], s, -jnp.inf)
        p = jax.nn.softmax(s, axis=-1).astype(vs.dtype)
        o = jnp.einsum('bqgrk,bqkgd->bqgrd', p, vs)
        return o.reshape(B, Q_BLOCK, ATTN_WIDTH)

    out = lax.map(block, (to_blocks(q), to_blocks(q_idx), to_blocks(w_idx), qpos))
    out = jnp.moveaxis(out, 0, 1).reshape(B, Tp, ATTN_WIDTH)[:, :T]
    return out


def setup_inputs(seed: int = 0) -> dict:
    key = jax.random.key(seed)
    ks = jax.random.split(key, 20)
    f32 = jnp.float32

    def w(k, shape, fan_in):
        return jax.random.normal(k, shape, f32) * (fan_in ** -0.5)

    def gain(k, shape):
        return 1.0 + 0.02 * jax.random.normal(k, shape, f32)

    L = DEPTH
    return {
        "x": jax.random.normal(ks[0], (BATCH, SEQ, D_MODEL), f32),
        "meta_tokens": jax.random.normal(ks[1], (N_META, D_MODEL), f32),
        "ffn1_norm_g": gain(ks[2], (L, D_MODEL)),
        "ffn1_w_gate": w(ks[3], (L, D_MODEL, D_FF), D_MODEL),
        "ffn1_w_up": w(ks[4], (L, D_MODEL, D_FF), D_MODEL),
        "ffn1_w_down": w(ks[5], (L, D_FF, D_MODEL), D_FF),
        "mix_norm_g": gain(ks[6], (L, D_MODEL)),
        "w_in": w(ks[7], (L, D_MODEL, IN_COLS), D_MODEL),
        "q_norm_g": gain(ks[8], (L, HEAD_DIM)),
        "k_norm_g": gain(ks[9], (L, HEAD_DIM)),
        "conv_w": w(ks[10], (L, CONV_K, CONV_WIDTH), CONV_K),
        "conv_b": 0.01 * jax.random.normal(ks[11], (L, CONV_WIDTH), f32),
        "w_attn_branch": w(ks[12], (L, ATTN_WIDTH, D_MODEL), ATTN_WIDTH),
        "w_conv_branch": w(ks[13], (L, CONV_WIDTH, D_MODEL), CONV_WIDTH),
        "w_out": w(ks[14], (L, D_MODEL, D_MODEL), D_MODEL),
        "ffn2_norm_g": gain(ks[15], (L, D_MODEL)),
        "ffn2_w_gate": w(ks[16], (L, D_MODEL, D_FF), D_MODEL),
        "ffn2_w_up": w(ks[17], (L, D_MODEL, D_FF), D_MODEL),
        "ffn2_w_down": w(ks[18], (L, D_FF, D_MODEL), D_FF),
    }


def reference(x, meta_tokens, ffn1_norm_g, ffn1_w_gate, ffn1_w_up, ffn1_w_down,
              mix_norm_g, w_in, q_norm_g, k_norm_g, conv_w, conv_b,
              w_attn_branch, w_conv_branch, w_out,
              ffn2_norm_g, ffn2_w_gate, ffn2_w_up, ffn2_w_down):
    B, S, D = x.shape
    meta = jnp.broadcast_to(meta_tokens[None].astype(x.dtype), (B, N_META, D))
    h = jnp.concatenate([meta, x], axis=1)
    T = h.shape[1]
    k_sel = min(TOPK_MAX, S // 4)
    cos_a, sin_a = rope_tables(T, HEAD_DIM // ROT_DIV)
    cos_i, sin_i = rope_tables(T, IDX_DIM // ROT_DIV)

    for l in range(DEPTH):
        u = rms_norm(h, ffn1_norm_g[l])
        h = h + 0.5 * swiglu(u, ffn1_w_gate[l], ffn1_w_up[l], ffn1_w_down[l])

        u = rms_norm(h, mix_norm_g[l])
        z = u @ w_in[l]
        q, k, v, qi, ki, wi, xc, gate_b, gate_c, ga, gc = jnp.split(z, SPLIT_POINTS, axis=-1)

        q = partial_rope(rms_norm(q.reshape(B, T, N_HEADS, HEAD_DIM), q_norm_g[l]), cos_a, sin_a)
        k = partial_rope(rms_norm(k.reshape(B, T, N_KV_HEADS, HEAD_DIM), k_norm_g[l]), cos_a, sin_a)
        v = v.reshape(B, T, N_KV_HEADS, HEAD_DIM)
        qi = partial_rope(qi.reshape(B, T, N_IDX_HEADS, IDX_DIM), cos_i, sin_i)
        ki = partial_rope(ki.reshape(B, T, 1, IDX_DIM), cos_i, sin_i)[:, :, 0]
        y_attn = dsa_sparse_attention(q, k, v, qi, ki, wi, k_sel) @ w_attn_branch[l]

        y_conv = (gate_b * causal_depthwise_conv(gate_c * xc, conv_w[l], conv_b[l])) @ w_conv_branch[l]

        merged = jax.nn.sigmoid(ga) * y_attn + jax.nn.sigmoid(gc) * y_conv
        h = h + merged @ w_out[l]

        u = rms_norm(h, ffn2_norm_g[l])
        h = h + 0.5 * swiglu(u, ffn2_w_gate[l], ffn2_w_up[l], ffn2_w_down[l])

    return h[:, N_META:]
```

```python
import functools

import jax
import jax.numpy as jnp
import numpy as np
from jax import lax
from jax.experimental import pallas as pl
from jax.experimental.pallas import tpu as pltpu

D_MODEL = 2048
N_META = 16
N_HEADS = 8
N_KV_HEADS = 2
HEAD_DIM = 128
ATTN_WIDTH = N_HEADS * HEAD_DIM
KV_WIDTH = N_KV_HEADS * HEAD_DIM
N_IDX_HEADS = 16
IDX_DIM = 64
TOPK_MAX = 256
CONV_WIDTH = D_MODEL // 2
CONV_K = 3
D_FF = 5632
ROPE_THETA = 500000.0
ROT_DIV = 4
EPS = 1e-6

LANES = 128
VMEM_LIMIT = 52 * 1024 * 1024
NEG_BIG = -1e30

FFN_TM, FFN_TF = 512, 512
PROJ_TM = 256
GATE_TM, GATE_TN = 1024, 512
MERGE_TM = 256
QB = 256
KC = 256
HALO = 16


def _params(sem):
    return pltpu.CompilerParams(dimension_semantics=sem, vmem_limit_bytes=VMEM_LIMIT)


def _rms(x, g):
    ms = jnp.mean(x * x, axis=-1, keepdims=True)
    return x * lax.rsqrt(ms + EPS) * g


def _ffn_kernel(h_ref, g_ref, wg_ref, wu_ref, wd_ref, o_ref, u_scr):
    j = pl.program_id(1)

    @pl.when(j == 0)
    def _():
        h = h_ref[...]
        u_scr[...] = _rms(h, g_ref[...]).astype(jnp.bfloat16)
        o_ref[...] = h

    u = u_scr[...]
    gate = jnp.dot(u, wg_ref[...], preferred_element_type=jnp.float32)
    up = jnp.dot(u, wu_ref[...], preferred_element_type=jnp.float32)
    a = (gate * jax.nn.sigmoid(gate)) * (up * 0.5)
    o_ref[...] += jnp.dot(a.astype(jnp.bfloat16), wd_ref[...], preferred_element_type=jnp.float32)


def _ffn(h, g, wg, wu, wd):
    n = h.shape[0]
    tm = min(FFN_TM, n)
    return pl.pallas_call(
        _ffn_kernel,
        out_shape=jax.ShapeDtypeStruct((n, D_MODEL), jnp.float32),
        grid=(n // tm, D_FF // FFN_TF),
        in_specs=[
            pl.BlockSpec((tm, D_MODEL), lambda i, j: (i, 0)),
            pl.BlockSpec((1, D_MODEL), lambda i, j: (0, 0)),
            pl.BlockSpec((D_MODEL, FFN_TF), lambda i, j: (0, j)),
            pl.BlockSpec((D_MODEL, FFN_TF), lambda i, j: (0, j)),
            pl.BlockSpec((FFN_TF, D_MODEL), lambda i, j: (j, 0)),
        ],
        out_specs=pl.BlockSpec((tm, D_MODEL), lambda i, j: (i, 0)),
        scratch_shapes=[pltpu.VMEM((tm, D_MODEL), jnp.bfloat16)],
        compiler_params=_params(("parallel", "arbitrary")),
        name="ffn",
    )(h, g, wg, wu, wd)


def _rope(x, cos, s_lo, s_hi, half):
    n = x.shape[-1]
    return x * cos + pltpu.roll(x, n - half, 1) * s_lo + pltpu.roll(x, half, 1) * s_hi


def _proj_attn_kernel(h_ref, g_ref, w_ref, gq_ref, gk_ref, tab_ref,
                      q_ref, qi_ref, k_ref, v_ref, ki_ref, wi_ref):
    u = _rms(h_ref[...], g_ref[...]).astype(jnp.bfloat16)
    z = jnp.dot(u, w_ref[...], preferred_element_type=jnp.float32)
    ca, sa_lo, sa_hi = tab_ref[0], tab_ref[1], tab_ref[2]
    ci, si_lo, si_hi = tab_ref[3], tab_ref[4], tab_ref[5]
    half_a = HEAD_DIM // ROT_DIV // 2
    half_i = IDX_DIM // ROT_DIV // 2
    gq = gq_ref[...]
    gk = gk_ref[...]
    for hd in range(N_HEADS):
        sl = slice(hd * LANES, (hd + 1) * LANES)
        q_ref[:, sl] = _rope(_rms(z[:, sl], gq), ca, sa_lo, sa_hi, half_a).astype(jnp.bfloat16)
    off = ATTN_WIDTH
    for p in range(N_IDX_HEADS * IDX_DIM // LANES):
        sl = slice(p * LANES, (p + 1) * LANES)
        x = z[:, off + p * LANES:off + (p + 1) * LANES]
        qi_ref[:, sl] = _rope(x, ci, si_lo, si_hi, half_i).astype(jnp.bfloat16)
    off += N_IDX_HEADS * IDX_DIM
    for hd in range(N_KV_HEADS):
        sl = slice(hd * LANES, (hd + 1) * LANES)
        x = z[:, off + hd * LANES:off + (hd + 1) * LANES]
        k_ref[:, sl] = _rope(_rms(x, gk), ca, sa_lo, sa_hi, half_a).astype(jnp.bfloat16)
    off += KV_WIDTH
    v_ref[...] = z[:, off:off + KV_WIDTH].astype(jnp.bfloat16)
    off += KV_WIDTH
    x = z[:, off:off + LANES]
    lane = lax.broadcasted_iota(jnp.int32, x.shape, 1)
    is_ki = lane < IDX_DIM
    roped = _rope(x, jnp.where(is_ki, ci, 1.0), jnp.where(is_ki, si_lo, 0.0),
                  jnp.where(is_ki, si_hi, 0.0), half_i)
    swapped = pltpu.roll(roped, IDX_DIM, 1)
    ki_ref[...] = jnp.where(is_ki, roped, swapped).astype(jnp.bfloat16)
    wi_ref[...] = swapped


def _proj_attn(h, g, w, gq, gk, tabs):
    n = h.shape[0]
    tm = min(PROJ_TM, n)
    cols = w.shape[1]
    nt = tabs.shape[1] // tm
    row = lambda i: (i, 0)
    const = lambda i: (0, 0)
    bf = jnp.bfloat16
    return pl.pallas_call(
        _proj_attn_kernel,
        out_shape=(
            jax.ShapeDtypeStruct((n, ATTN_WIDTH), bf),
            jax.ShapeDtypeStruct((n, N_IDX_HEADS * IDX_DIM), bf),
            jax.ShapeDtypeStruct((n, KV_WIDTH), bf),
            jax.ShapeDtypeStruct((n, KV_WIDTH), bf),
            jax.ShapeDtypeStruct((n, LANES), bf),
            jax.ShapeDtypeStruct((n, LANES), jnp.float32),
        ),
        grid=(n // tm,),
        in_specs=[
            pl.BlockSpec((tm, D_MODEL), row),
            pl.BlockSpec((1, D_MODEL), const),
            pl.BlockSpec((D_MODEL, cols), const),
            pl.BlockSpec((1, LANES), const),
            pl.BlockSpec((1, LANES), const),
            pl.BlockSpec((6, tm, LANES), lambda i: (0, i % nt, 0)),
        ],
        out_specs=(
            pl.BlockSpec((tm, ATTN_WIDTH), row),
            pl.BlockSpec((tm, N_IDX_HEADS * IDX_DIM), row),
            pl.BlockSpec((tm, KV_WIDTH), row),
            pl.BlockSpec((tm, KV_WIDTH), row),
            pl.BlockSpec((tm, LANES), row),
            pl.BlockSpec((tm, LANES), row),
        ),
        compiler_params=_params(("parallel",)),
        name="proj_attn",
    )(h, g, w, gq, gk, tabs)


def _proj_gate_kernel(h_ref, g_ref, w_ref, o_ref, u_scr):
    @pl.when(pl.program_id(1) == 0)
    def _():
        u_scr[...] = _rms(h_ref[...], g_ref[...]).astype(jnp.bfloat16)

    o_ref[...] = jnp.dot(u_scr[...], w_ref[...],
                         preferred_element_type=jnp.float32).astype(o_ref.dtype)


def _proj_gate(h, g, w):
    n = h.shape[0]
    tm = min(GATE_TM, n)
    cols = w.shape[1]
    return pl.pallas_call(
        _proj_gate_kernel,
        out_shape=jax.ShapeDtypeStruct((n, cols), jnp.bfloat16),
        grid=(n // tm, cols // GATE_TN),
        in_specs=[
            pl.BlockSpec((tm, D_MODEL), lambda i, j: (i, 0)),
            pl.BlockSpec((1, D_MODEL), lambda i, j: (0, 0)),
            pl.BlockSpec((D_MODEL, GATE_TN), lambda i, j: (0, j)),
        ],
        out_specs=pl.BlockSpec((tm, GATE_TN), lambda i, j: (i, j)),
        scratch_shapes=[pltpu.VMEM((tm, D_MODEL), jnp.bfloat16)],
        compiler_params=_params(("parallel", "arbitrary")),
        name="proj_gate",
    )(h, g, w)


def _dot_nt(a, b):
    return lax.dot_general(a, b, (((1,), (1,)), ((), ())), preferred_element_type=jnp.float32)


def _attn_kernel(k_sel, q_ref, qi_ref, wi_ref, k_ref, v_ref, ki_ref, km_ref, vm_ref, kim_ref,
                 o_ref, qs_scr, qis_scr, sct_scr, bias_scr, m_scr, l_scr, acc_scr):
    qb = pl.program_id(1)
    n_chunks = qb + 2
    rep = N_HEADS // N_KV_HEADS
    hpg = 4
    lane_q = lax.broadcasted_iota(jnp.int32, (QB, LANES), 1)
    neg_inf = float("-inf")

    for g in range(N_KV_HEADS):
        for r in range(rep):
            hd = g * rep + r
            qs_scr[g, r * QB:(r + 1) * QB, :] = q_ref[0, :, hd * LANES:(hd + 1) * LANES]
    for i in range(N_IDX_HEADS // hpg):
        for r in range(hpg):
            hd = i * hpg + r
            pair = qi_ref[0, :, (hd // 2) * LANES:(hd // 2 + 1) * LANES]
            keep = (lane_q < IDX_DIM) if hd % 2 == 0 else (lane_q >= IDX_DIM)
            qis_scr[i, r * QB:(r + 1) * QB, :] = jnp.where(keep, pair, jnp.zeros_like(pair))

    w_all = wi_ref[0]
    row = lax.broadcasted_iota(jnp.int32, (QB, KC), 0)
    lane = lax.broadcasted_iota(jnp.int32, (QB, KC), 1)

    def index_scores(ki_c):
        score = jnp.zeros((QB, KC), jnp.float32)
        for i in range(N_IDX_HEADS // hpg):
            d = _dot_nt(qis_scr[i], ki_c)
            for r in range(hpg):
                hd = i * hpg + r
                score = score + jnp.maximum(d[r * QB:(r + 1) * QB], 0.0) * w_all[:, hd:hd + 1]
        return score

    sct_scr[0] = jnp.where(lane < N_META, index_scores(kim_ref[...]), neg_inf).T

    def score_chunk(c, carry):
        ki_c = ki_ref[0, pl.ds(pl.multiple_of(c * KC, KC), KC), :]
        causal = (c - qb) * KC + lane <= row
        sct_scr[c + 1] = jnp.where(causal, index_scores(ki_c), neg_inf).T
        return carry

    lax.fori_loop(0, qb + 1, score_chunk, 0)

    kf = float(k_sel)
    qlane = lax.broadcasted_iota(jnp.int32, (1, QB), 1)
    n_valid = (N_META + 1 + qb * QB + qlane).astype(jnp.float32)

    def minmax(c, carry):
        mn, mx = carry
        x = sct_scr[c]
        mx = jnp.maximum(mx, jnp.max(x, axis=0, keepdims=True))
        mn = jnp.minimum(mn, jnp.min(jnp.where(x == neg_inf, float("inf"), x), axis=0,
                                     keepdims=True))
        return mn, mx

    mn, mx = lax.fori_loop(0, n_chunks, minmax,
                           (jnp.full((1, QB), float("inf"), jnp.float32),
                            jnp.full((1, QB), neg_inf, jnp.float32)))

    def count_ge(t):
        def body(c, acc):
            return acc + jnp.sum(jnp.where(sct_scr[c] >= t, 1.0, 0.0), axis=0, keepdims=True)
        return lax.fori_loop(0, n_chunks, body, jnp.zeros((1, QB), jnp.float32))

    def midpoint(lo, hi):
        return lo + (hi - lo) * 0.5

    def is_active(lo, hi, cnt_lo):
        mid = midpoint(lo, hi)
        return jnp.where((cnt_lo != kf) & (mid > lo) & (mid < hi), 1.0, 0.0)

    cnt_mx = count_ge(mx)
    at_max = cnt_mx >= kf
    lo0 = jnp.where(at_max, mx, mn)
    cnt0 = jnp.where(at_max, cnt_mx, n_valid)
    act0 = jnp.where(n_valid > kf, is_active(lo0, mx, cnt0), 0.0)

    def bisect(state):
        lo, hi, cnt_lo, act = state
        mid = midpoint(lo, hi)
        cnt = count_ge(mid)
        up = (act > 0.5) & (cnt >= kf)
        down = (act > 0.5) & (cnt < kf)
        lo = jnp.where(up, mid, lo)
        cnt_lo = jnp.where(up, cnt, cnt_lo)
        hi = jnp.where(down, mid, hi)
        return lo, hi, cnt_lo, act * is_active(lo, hi, cnt_lo)

    thr, _, cnt_thr, _ = lax.while_loop(lambda st: jnp.max(st[3]) > 0.5, bisect,
                                        (lo0, mx, cnt0, act0))
    has_tie = cnt_thr > kf
    any_tie = jnp.max(jnp.where(has_tie, 1.0, 0.0)) > 0.5

    @pl.when(jnp.logical_not(any_tie))
    def _():
        def write_bias(c, carry):
            bias_scr[c] = jnp.where(sct_scr[c] >= thr, 0.0, NEG_BIG).T
            return carry
        lax.fori_loop(0, n_chunks, write_bias, 0)

    @pl.when(any_tie)
    def _():
        key_pos = lax.broadcasted_iota(jnp.int32, (KC, QB), 0)

        def count_gt(c, acc):
            return acc + jnp.sum(jnp.where(sct_scr[c] > thr, 1.0, 0.0), axis=0, keepdims=True)
        need = kf - lax.fori_loop(0, n_chunks, count_gt, jnp.zeros((1, QB), jnp.float32))

        def count_tie_below(lim):
            def body(c, acc):
                hit = jnp.where(sct_scr[c] == thr, c * KC + key_pos, 2 ** 30) < lim
                return acc + jnp.sum(jnp.where(hit, 1.0, 0.0), axis=0, keepdims=True)
            return lax.fori_loop(0, n_chunks, body, jnp.zeros((1, QB), jnp.float32))

        def pos_step(i, last):
            cand = last | lax.shift_left(jnp.int32(1), 11 - i)
            return jnp.where(count_tie_below(cand) < need, cand, last)

        last = lax.fori_loop(0, 12, pos_step, jnp.zeros((1, QB), jnp.int32))
        last = jnp.where(has_tie, last, 2 ** 30)

        def write_bias(c, carry):
            x = sct_scr[c]
            tie_pos = jnp.where(x == thr, c * KC + key_pos, 2 ** 30 + 1)
            keep = jnp.where(x > thr, 1.0, jnp.where(tie_pos <= last, 1.0, 0.0))
            bias_scr[c] = jnp.where(keep > 0.5, 0.0, NEG_BIG).T
            return carry
        lax.fori_loop(0, n_chunks, write_bias, 0)

    m_scr[...] = jnp.full(m_scr.shape, NEG_BIG, jnp.float32)
    l_scr[...] = jnp.zeros(l_scr.shape, jnp.float32)
    acc_scr[...] = jnp.zeros(acc_scr.shape, jnp.float32)
    scale = HEAD_DIM ** -0.5

    def attend(c, k_c, v_c):
        bias = bias_scr[c]
        for g in range(N_KV_HEADS):
            sl = slice(g * LANES, (g + 1) * LANES)
            s = _dot_nt(qs_scr[g], k_c[:, sl]) * scale
            s = jnp.concatenate([s[r * QB:(r + 1) * QB] + bias for r in range(rep)], axis=0)
            m_prev = m_scr[g]
            m_new = jnp.maximum(m_prev, jnp.max(s, axis=-1, keepdims=True))
            alpha = jnp.exp(m_prev - m_new)
            p = jnp.exp(s - jnp.concatenate([m_new] * (KC // LANES), axis=1))
            l_scr[g] = alpha * l_scr[g] + jnp.sum(p, axis=-1, keepdims=True)
            acc_scr[g] = alpha * acc_scr[g] + jnp.dot(
                p.astype(jnp.bfloat16), v_c[:, sl], preferred_element_type=jnp.float32)
            m_scr[g] = m_new

    attend(0, km_ref[...], vm_ref[...])

    def attend_chunk(c, carry):
        rows = pl.ds(pl.multiple_of(c * KC, KC), KC)
        attend(c + 1, k_ref[0, rows, :], v_ref[0, rows, :])
        return carry

    lax.fori_loop(0, qb + 1, attend_chunk, 0)

    for g in range(N_KV_HEADS):
        o = acc_scr[g] / l_scr[g]
        for r in range(rep):
            hd = g * rep + r
            o_ref[0, :, hd * LANES:(hd + 1) * LANES] = o[r * QB:(r + 1) * QB].astype(o_ref.dtype)


def _attention(q, qi, wi, k, v, ki, km, vm, kim, k_sel):
    b, s, _ = q.shape
    blk = lambda bi, qi_: (bi, qi_, 0)
    full = lambda bi, qi_: (bi, 0, 0)
    const = lambda bi, qi_: (0, 0)
    rep = N_HEADS // N_KV_HEADS
    return pl.pallas_call(
        functools.partial(_attn_kernel, k_sel),
        out_shape=jax.ShapeDtypeStruct((b, s, ATTN_WIDTH), jnp.bfloat16),
        grid=(b, s // QB),
        in_specs=[
            pl.BlockSpec((1, QB, ATTN_WIDTH), blk),
            pl.BlockSpec((1, QB, N_IDX_HEADS * IDX_DIM), blk),
            pl.BlockSpec((1, QB, LANES), blk),
            pl.BlockSpec((1, s, KV_WIDTH), full),
            pl.BlockSpec((1, s, KV_WIDTH), full),
            pl.BlockSpec((1, s, LANES), full),
            pl.BlockSpec((KC, KV_WIDTH), const),
            pl.BlockSpec((KC, KV_WIDTH), const),
            pl.BlockSpec((KC, LANES), const),
        ],
        out_specs=pl.BlockSpec((1, QB, ATTN_WIDTH), blk),
        scratch_shapes=[
            pltpu.VMEM((N_KV_HEADS, rep * QB, LANES), jnp.bfloat16),
            pltpu.VMEM((N_IDX_HEADS // 4, 4 * QB, LANES), jnp.bfloat16),
            pltpu.VMEM((s // KC + 1, KC, QB), jnp.float32),
            pltpu.VMEM((s // KC + 1, QB, KC), jnp.float32),
            pltpu.VMEM((N_KV_HEADS, rep * QB, LANES), jnp.float32),
            pltpu.VMEM((N_KV_HEADS, rep * QB, LANES), jnp.float32),
            pltpu.VMEM((N_KV_HEADS, rep * QB, LANES), jnp.float32),
        ],
        compiler_params=_params(("parallel", "arbitrary")),
        name="dsa_attention",
    )(q, qi, wi, k, v, ki, km, vm, kim)


def _merge_kernel(tiles_per_seq, h_ref, attn_ref, ga_ref, gc_ref, xc_ref, gb_ref, cg_ref,
                  xc_prev_ref, cg_prev_ref, xc_meta_ref, cg_meta_ref, cw_ref, cb_ref,
                  wa_ref, wc_ref, wo_ref, o_ref):
    f32 = jnp.float32
    first = pl.program_id(0) % tiles_per_seq == 0
    gx = cg_ref[...].astype(f32) * xc_ref[...].astype(f32)
    halo_prev = cg_prev_ref[...].astype(f32) * xc_prev_ref[...].astype(f32)
    halo_meta = cg_meta_ref[...].astype(f32) * xc_meta_ref[...].astype(f32)
    halo = jnp.where(first, halo_meta, halo_prev)
    h1, h2 = halo[HALO - 1:HALO], halo[HALO - 2:HALO - 1]
    row = lax.broadcasted_iota(jnp.int32, gx.shape, 0)
    prev1 = jnp.where(row == 0, h1, pltpu.roll(gx, 1, 0))
    prev2 = jnp.where(row == 0, h2, jnp.where(row == 1, h1, pltpu.roll(gx, 2, 0)))
    cw = cw_ref[...]
    conv = cw[0:1] * prev2 + cw[1:2] * prev1 + cw[2:3] * gx + cb_ref[...]
    feat = (gb_ref[...].astype(f32) * conv).astype(jnp.bfloat16)
    y_conv = jnp.dot(feat, wc_ref[...], preferred_element_type=f32)
    y_attn = jnp.dot(attn_ref[...], wa_ref[...], preferred_element_type=f32)
    merged = (jax.nn.sigmoid(ga_ref[...].astype(f32)) * y_attn
              + jax.nn.sigmoid(gc_ref[...].astype(f32)) * y_conv)
    o_ref[...] = h_ref[...] + jnp.dot(merged.astype(jnp.bfloat16), wo_ref[...],
                                      preferred_element_type=f32)


def _merge(h, attn, zg, zg_meta, conv_w, conv_b, wa, wc, wo, seq):
    n = h.shape[0]
    tm = MERGE_TM
    tiles_per_seq = seq // tm
    cw_blocks = D_MODEL // CONV_WIDTH
    sub = HALO
    row2 = lambda i: (i, 0)
    const = lambda i: (0, 0)
    col = lambda c: (lambda i: (i, c))
    prev = lambda c: (lambda i: (jnp.maximum(i * (tm // sub) - 1, 0), c))
    meta = lambda c: (lambda i: (N_META // sub - 1, c))
    single = pl.Buffered(1)
    return pl.pallas_call(
        functools.partial(_merge_kernel, tiles_per_seq),
        out_shape=jax.ShapeDtypeStruct((n, D_MODEL), jnp.float32),
        grid=(n // tm,),
        in_specs=[
            pl.BlockSpec((tm, D_MODEL), row2),
            pl.BlockSpec((tm, ATTN_WIDTH), row2),
            pl.BlockSpec((tm, D_MODEL), col(0)),
            pl.BlockSpec((tm, D_MODEL), col(1)),
            pl.BlockSpec((tm, CONV_WIDTH), col(2 * cw_blocks)),
            pl.BlockSpec((tm, CONV_WIDTH), col(2 * cw_blocks + 1)),
            pl.BlockSpec((tm, CONV_WIDTH), col(2 * cw_blocks + 2)),
            pl.BlockSpec((sub, CONV_WIDTH), prev(2 * cw_blocks)),
            pl.BlockSpec((sub, CONV_WIDTH), prev(2 * cw_blocks + 2)),
            pl.BlockSpec((sub, CONV_WIDTH), meta(2 * cw_blocks)),
            pl.BlockSpec((sub, CONV_WIDTH), meta(2 * cw_blocks + 2)),
            pl.BlockSpec((CONV_K, CONV_WIDTH), const),
            pl.BlockSpec((1, CONV_WIDTH), const),
            pl.BlockSpec((ATTN_WIDTH, D_MODEL), const, pipeline_mode=single),
            pl.BlockSpec((CONV_WIDTH, D_MODEL), const, pipeline_mode=single),
            pl.BlockSpec((D_MODEL, D_MODEL), const, pipeline_mode=single),
        ],
        out_specs=pl.BlockSpec((tm, D_MODEL), row2),
        compiler_params=_params(("parallel",)),
        name="merge",
    )(h, attn, zg, zg, zg, zg, zg, zg, zg, zg_meta, zg_meta, conv_w, conv_b, wa, wc, wo)


def _rope_tables(pos, head_dim):
    rot = head_dim // ROT_DIV
    half = rot // 2
    inv = ROPE_THETA ** (-jnp.arange(0, rot, 2, dtype=jnp.float32) / rot)
    ang = pos.astype(jnp.float32)[:, None] * inv[None, :]
    cos, sin = jnp.cos(ang), jnp.sin(ang)
    n = pos.shape[0]
    pad = head_dim - rot
    c = jnp.concatenate([cos, cos, jnp.ones((n, pad), jnp.float32)], axis=1)
    zeros_h = jnp.zeros((n, half), jnp.float32)
    zeros_p = jnp.zeros((n, pad), jnp.float32)
    s_lo = jnp.concatenate([-sin, zeros_h, zeros_p], axis=1)
    s_hi = jnp.concatenate([zeros_h, sin, zeros_p], axis=1)
    tile = LANES // head_dim
    return [jnp.tile(t, (1, tile)) for t in (c, s_lo, s_hi)]


def kernel(x, meta_tokens, ffn1_norm_g, ffn1_w_gate, ffn1_w_up, ffn1_w_down, mix_norm_g, w_in, q_norm_g, k_norm_g, conv_w, conv_b, w_attn_branch, w_conv_branch, w_out, ffn2_norm_g, ffn2_w_gate, ffn2_w_up, ffn2_w_down):
    bsz, seq, d = x.shape
    bf = jnp.bfloat16
    k_sel = min(TOPK_MAX, seq // 4)
    assert d == D_MODEL and seq % QB == 0 and k_sel <= KC

    widths = [ATTN_WIDTH, KV_WIDTH, KV_WIDTH, N_IDX_HEADS * IDX_DIM, IDX_DIM, N_IDX_HEADS,
              CONV_WIDTH, CONV_WIDTH, CONV_WIDTH, D_MODEL, D_MODEL]
    offs = np.concatenate([[0], np.cumsum(widths)])
    seg = lambda w, i: w[:, int(offs[i]):int(offs[i + 1])]

    h = x.reshape(bsz * seq, d)
    hm = meta_tokens.astype(x.dtype)
    pos_seq = jnp.arange(N_META, N_META + seq, dtype=jnp.int32)
    pos_meta = jnp.arange(N_META, dtype=jnp.int32)
    tabs_seq = jnp.stack(_rope_tables(pos_seq, HEAD_DIM) + _rope_tables(pos_seq, IDX_DIM))
    tabs_meta = jnp.stack(_rope_tables(pos_meta, HEAD_DIM) + _rope_tables(pos_meta, IDX_DIM))

    for l in range(ffn1_norm_g.shape[0]):
        wl = w_in[l]
        pad = jnp.zeros((d, LANES - IDX_DIM - N_IDX_HEADS), wl.dtype)
        w_a = jnp.concatenate([seg(wl, 0), seg(wl, 3), seg(wl, 1), seg(wl, 2), seg(wl, 4),
                               seg(wl, 5), pad], axis=1).astype(bf)
        w_g = jnp.concatenate([seg(wl, 9), seg(wl, 10), seg(wl, 6), seg(wl, 7), seg(wl, 8)],
                              axis=1).astype(bf)
        g1 = ffn1_norm_g[l][None]
        gm = mix_norm_g[l][None]
        wg1, wu1, wd1 = ffn1_w_gate[l].astype(bf), ffn1_w_up[l].astype(bf), ffn1_w_down[l].astype(bf)

        h = _ffn(h, g1, wg1, wu1, wd1)
        hm = _ffn(hm, g1, wg1, wu1, wd1)

        gq, gk = q_norm_g[l][None], k_norm_g[l][None]
        q, qi, k, v, ki, wi = _proj_attn(h, gm, w_a, gq, gk, tabs_seq)
        _, _, km, vm, kim, _ = _proj_attn(hm, gm, w_a, gq, gk, tabs_meta)
        zg = _proj_gate(h, gm, w_g)
        zg_meta = _proj_gate(hm, gm, w_g)

        pad_rows = lambda a: jnp.pad(a, ((0, KC - N_META), (0, 0)))
        r3 = lambda a: a.reshape(bsz, seq, a.shape[-1])
        attn = _attention(r3(q), r3(qi), r3(wi), r3(k), r3(v), r3(ki),
                          pad_rows(km), pad_rows(vm), pad_rows(kim), k_sel)
        attn = attn.reshape(bsz * seq, ATTN_WIDTH)

        h = _merge(h, attn, zg, zg_meta, conv_w[l], conv_b[l][None],
                   w_attn_branch[l].astype(bf), w_conv_branch[l].astype(bf), w_out[l].astype(bf),
                   seq)

        h = _ffn(h, ffn2_norm_g[l][None], ffn2_w_gate[l].astype(bf), ffn2_w_up[l].astype(bf),
                 ffn2_w_down[l].astype(bf))

    return h.reshape(bsz, seq, d)
```

```python
import functools
import math

import jax
import jax.numpy as jnp
import numpy as np
from jax import lax
from jax.experimental import pallas as pl
from jax.experimental.pallas import tpu as pltpu

D_MODEL = 2048
N_META = 16
N_HEADS = 8
N_KV_HEADS = 2
HEAD_DIM = 128
ATTN_WIDTH = N_HEADS * HEAD_DIM
KV_WIDTH = N_KV_HEADS * HEAD_DIM
N_IDX_HEADS = 16
IDX_DIM = 64
IDX_WIDTH = N_IDX_HEADS * IDX_DIM
TOPK_MAX = 256
CONV_WIDTH = D_MODEL // 2
CONV_K = 3
D_FF = 5632
ROPE_THETA = 500000.0
ROT_DIV = 4
EPS = 1e-6

LANES = 128
VMEM_LIMIT = 52 * 1024 * 1024
NEG_BIG = -1e30
Q_SCALE = math.log2(math.e) / math.sqrt(HEAD_DIM)

FFN_TM, FFN_TF = 512, 512
PROJ_TM = 256
GATE_TM, GATE_TN = 1024, 1024
MERGE_TM = 256
QB = 256
KC = 256
MW = 128
HALO = 16


def _params(sem):
    return pltpu.CompilerParams(dimension_semantics=sem, vmem_limit_bytes=VMEM_LIMIT)


def _rms(x, g):
    ms = jnp.mean(x * x, axis=-1, keepdims=True)
    return x * lax.rsqrt(ms + EPS) * g


def _mm(a, b):
    return jnp.dot(a, b, preferred_element_type=jnp.float32)


def _ffn_kernel(h_ref, g_ref, wg_ref, wu_ref, wd_ref, o_ref, u_scr):
    j = pl.program_id(1)

    @pl.when(j == 0)
    def _():
        h = h_ref[...]
        u_scr[...] = _rms(h, g_ref[...]).astype(jnp.bfloat16)
        o_ref[...] = h

    u = u_scr[...]
    gate = _mm(u, wg_ref[...])
    up = _mm(u, wu_ref[...])
    a = (gate * jax.nn.sigmoid(gate)) * (up * 0.5)
    o_ref[...] += _mm(a.astype(jnp.bfloat16), wd_ref[...])


def _ffn(h, g, wg, wu, wd):
    n = h.shape[0]
    tm = min(FFN_TM, n)
    return pl.pallas_call(
        _ffn_kernel,
        out_shape=jax.ShapeDtypeStruct((n, D_MODEL), jnp.float32),
        grid=(n // tm, D_FF // FFN_TF),
        in_specs=[
            pl.BlockSpec((tm, D_MODEL), lambda i, j: (i, 0)),
            pl.BlockSpec((1, D_MODEL), lambda i, j: (0, 0)),
            pl.BlockSpec((D_MODEL, FFN_TF), lambda i, j: (0, j)),
            pl.BlockSpec((D_MODEL, FFN_TF), lambda i, j: (0, j)),
            pl.BlockSpec((FFN_TF, D_MODEL), lambda i, j: (j, 0)),
        ],
        out_specs=pl.BlockSpec((tm, D_MODEL), lambda i, j: (i, 0)),
        scratch_shapes=[pltpu.VMEM((tm, D_MODEL), jnp.bfloat16)],
        compiler_params=_params(("parallel", "arbitrary")),
        name="ffn",
    )(h, g, wg, wu, wd)


def _rope(x, cos, s_lo, s_hi, half):
    n = x.shape[-1]
    return x * cos + pltpu.roll(x, n - half, 1) * s_lo + pltpu.roll(x, half, 1) * s_hi


def _proj_attn_kernel(h_ref, g_ref, wq_ref, wqi_ref, wk_ref, wv_ref, wkw_ref, gq_ref, gk_ref,
                      tab_ref, q_ref, qi_ref, k_ref, v_ref, ki_ref, wi_ref):
    bf = jnp.bfloat16
    u = _rms(h_ref[...], g_ref[...]).astype(bf)
    ca, sa_lo, sa_hi = tab_ref[0], tab_ref[1], tab_ref[2]
    ci, si_lo, si_hi = tab_ref[3], tab_ref[4], tab_ref[5]
    half_a = HEAD_DIM // ROT_DIV // 2
    half_i = IDX_DIM // ROT_DIV // 2
    gq = gq_ref[...]
    gk = gk_ref[...]
    z = _mm(u, wq_ref[...])
    for hd in range(N_HEADS):
        sl = slice(hd * LANES, (hd + 1) * LANES)
        q = _rope(_rms(z[:, sl], gq), ca, sa_lo, sa_hi, half_a)
        q_ref[:, sl] = (q * Q_SCALE).astype(bf)
    z = _mm(u, wqi_ref[...])
    for p in range(IDX_WIDTH // LANES):
        sl = slice(p * LANES, (p + 1) * LANES)
        qi_ref[:, sl] = _rope(z[:, sl], ci, si_lo, si_hi, half_i).astype(bf)
    z = _mm(u, wk_ref[...])
    for hd in range(N_KV_HEADS):
        sl = slice(hd * LANES, (hd + 1) * LANES)
        k_ref[:, sl] = _rope(_rms(z[:, sl], gk), ca, sa_lo, sa_hi, half_a).astype(bf)
    v_ref[...] = _mm(u, wv_ref[...]).astype(bf)
    x = _mm(u, wkw_ref[...])
    lane = lax.broadcasted_iota(jnp.int32, x.shape, 1)
    is_ki = lane < IDX_DIM
    roped = _rope(x, jnp.where(is_ki, ci, 1.0), jnp.where(is_ki, si_lo, 0.0),
                  jnp.where(is_ki, si_hi, 0.0), half_i)
    swapped = pltpu.roll(roped, IDX_DIM, 1)
    ki_ref[...] = jnp.where(is_ki, roped, swapped).astype(bf)
    wi_ref[...] = swapped


def _proj_attn(h, g, wq, wqi, wk, wv, wkw, gq, gk, tabs):
    n = h.shape[0]
    tm = min(PROJ_TM, n)
    nt = tabs.shape[1] // tm
    row = lambda i: (i, 0)
    const = lambda i: (0, 0)
    bf = jnp.bfloat16
    weight = lambda w: pl.BlockSpec(w.shape, const)
    return pl.pallas_call(
        _proj_attn_kernel,
        out_shape=(
            jax.ShapeDtypeStruct((n, ATTN_WIDTH), bf),
            jax.ShapeDtypeStruct((n, IDX_WIDTH), bf),
            jax.ShapeDtypeStruct((n, KV_WIDTH), bf),
            jax.ShapeDtypeStruct((n, KV_WIDTH), bf),
            jax.ShapeDtypeStruct((n, LANES), bf),
            jax.ShapeDtypeStruct((n, LANES), jnp.float32),
        ),
        grid=(n // tm,),
        in_specs=[
            pl.BlockSpec((tm, D_MODEL), row),
            pl.BlockSpec((1, D_MODEL), const),
            weight(wq), weight(wqi), weight(wk), weight(wv), weight(wkw),
            pl.BlockSpec((1, LANES), const),
            pl.BlockSpec((1, LANES), const),
            pl.BlockSpec((6, tm, LANES), lambda i: (0, i % nt, 0)),
        ],
        out_specs=(
            pl.BlockSpec((tm, ATTN_WIDTH), row),
            pl.BlockSpec((tm, IDX_WIDTH), row),
            pl.BlockSpec((tm, KV_WIDTH), row),
            pl.BlockSpec((tm, KV_WIDTH), row),
            pl.BlockSpec((tm, LANES), row),
            pl.BlockSpec((tm, LANES), row),
        ),
        compiler_params=_params(("parallel",)),
        name="proj_attn",
    )(h, g, wq, wqi, wk, wv, wkw, gq, gk, tabs)


def _proj_gate_kernel(h_ref, g_ref, w_ref, o_ref, u_scr):
    @pl.when(pl.program_id(1) == 0)
    def _():
        u_scr[...] = _rms(h_ref[...], g_ref[...]).astype(jnp.bfloat16)

    o_ref[...] = _mm(u_scr[...], w_ref[...]).astype(o_ref.dtype)


def _proj_gate(h, g, w):
    n = h.shape[0]
    tm = min(GATE_TM, n)
    cols = w.shape[1]
    return pl.pallas_call(
        _proj_gate_kernel,
        out_shape=jax.ShapeDtypeStruct((n, cols), jnp.bfloat16),
        grid=(n // tm, cols // GATE_TN),
        in_specs=[
            pl.BlockSpec((tm, D_MODEL), lambda i, j: (i, 0)),
            pl.BlockSpec((1, D_MODEL), lambda i, j: (0, 0)),
            pl.BlockSpec((D_MODEL, GATE_TN), lambda i, j: (0, j)),
        ],
        out_specs=pl.BlockSpec((tm, GATE_TN), lambda i, j: (i, j)),
        scratch_shapes=[pltpu.VMEM((tm, D_MODEL), jnp.bfloat16)],
        compiler_params=_params(("parallel", "arbitrary")),
        name="proj_gate",
    )(h, g, w)


def _dot_nt(a, b):
    return lax.dot_general(a, b, (((1,), (1,)), ((), ())), preferred_element_type=jnp.float32)


def _attn_kernel(k_sel, q_ref, qi_ref, wi_ref, k_ref, v_ref, ki_ref, km_ref, vm_ref, kim_ref,
                 o_ref, qs_scr, qis_scr, sctm_scr, sct_scr, biasm_scr, bias_scr,
                 m_scr, l_scr, acc_scr):
    qb = pl.program_id(1)
    n_seq = qb + 1
    rep = N_HEADS // N_KV_HEADS
    hpg = 4
    lane_q = lax.broadcasted_iota(jnp.int32, (QB, LANES), 1)
    neg_inf = float("-inf")

    for g in range(N_KV_HEADS):
        for r in range(rep):
            hd = g * rep + r
            qs_scr[g, r * QB:(r + 1) * QB, :] = q_ref[0, :, hd * LANES:(hd + 1) * LANES]
    for i in range(N_IDX_HEADS // hpg):
        for r in range(hpg):
            hd = i * hpg + r
            pair = qi_ref[0, :, (hd // 2) * LANES:(hd // 2 + 1) * LANES]
            keep = (lane_q < IDX_DIM) if hd % 2 == 0 else (lane_q >= IDX_DIM)
            qis_scr[i, r * QB:(r + 1) * QB, :] = jnp.where(keep, pair, jnp.zeros_like(pair))

    w_all = wi_ref[0]

    def index_scores(ki_c):
        kw = ki_c.shape[0]
        score = jnp.zeros((QB, kw), jnp.float32)
        for i in range(N_IDX_HEADS // hpg):
            d = _dot_nt(qis_scr[i], ki_c)
            for r in range(hpg):
                hd = i * hpg + r
                score = score + jnp.maximum(d[r * QB:(r + 1) * QB], 0.0) * w_all[:, hd:hd + 1]
        return score

    lane_m = lax.broadcasted_iota(jnp.int32, (QB, MW), 1)
    sctm_scr[...] = jnp.where(lane_m < N_META, index_scores(kim_ref[...]), neg_inf).T
    row = lax.broadcasted_iota(jnp.int32, (QB, KC), 0)
    lane = lax.broadcasted_iota(jnp.int32, (QB, KC), 1)

    def score_chunk(c, carry):
        ki_c = ki_ref[0, pl.ds(pl.multiple_of(c * KC, KC), KC), :]
        causal = (c - qb) * KC + lane <= row
        sct_scr[c] = jnp.where(causal, index_scores(ki_c), neg_inf).T
        return carry

    lax.fori_loop(0, n_seq, score_chunk, 0)

    def over_keys(f, init):
        acc = f(sctm_scr[...], 0, init)
        return lax.fori_loop(0, n_seq, lambda c, a: f(sct_scr[c], MW + c * KC, a), acc)

    def fold8(x):
        return jnp.sum(x.reshape(x.shape[0] // 8, 8, QB), axis=0)

    def count(pred):
        part = over_keys(lambda x, p0, a: a + fold8(jnp.where(pred(x, p0), 1.0, 0.0)),
                         jnp.zeros((8, QB), jnp.float32))
        return jnp.sum(part, axis=0, keepdims=True)

    kf = float(k_sel)
    qlane = lax.broadcasted_iota(jnp.int32, (1, QB), 1)
    n_valid = (N_META + 1 + qb * QB + qlane).astype(jnp.float32)

    def minmax(x, p0, carry):
        mn, mx = carry
        x3 = x.reshape(x.shape[0] // 8, 8, QB)
        mx = jnp.maximum(mx, jnp.max(x3, axis=0))
        mn = jnp.minimum(mn, jnp.min(jnp.where(x3 == neg_inf, float("inf"), x3), axis=0))
        return mn, mx

    mn, mx = over_keys(minmax, (jnp.full((8, QB), float("inf"), jnp.float32),
                                jnp.full((8, QB), neg_inf, jnp.float32)))
    mn = jnp.min(mn, axis=0, keepdims=True)
    mx = jnp.max(mx, axis=0, keepdims=True)

    def count_ge(t):
        return count(lambda x, p0: x >= t)

    def midpoint(lo, hi):
        return lo + (hi - lo) * 0.5

    def is_active(lo, hi, cnt_lo):
        mid = midpoint(lo, hi)
        return jnp.where((cnt_lo != kf) & (mid > lo) & (mid < hi), 1.0, 0.0)

    cnt_mx = count_ge(mx)
    at_max = cnt_mx >= kf
    lo0 = jnp.where(at_max, mx, mn)
    cnt0 = jnp.where(at_max, cnt_mx, n_valid)
    act0 = jnp.where(n_valid > kf, is_active(lo0, mx, cnt0), 0.0)

    def bisect_once(state):
        lo, hi, cnt_lo, act = state
        mid = midpoint(lo, hi)
        cnt = count_ge(mid)
        up = (act > 0.5) & (cnt >= kf)
        down = (act > 0.5) & (cnt < kf)
        lo = jnp.where(up, mid, lo)
        cnt_lo = jnp.where(up, cnt, cnt_lo)
        hi = jnp.where(down, mid, hi)
        return lo, hi, cnt_lo, act * is_active(lo, hi, cnt_lo)

    thr, _, cnt_thr, _ = lax.while_loop(lambda st: jnp.max(st[3]) > 0.5,
                                        lambda st: bisect_once(bisect_once(st)),
                                        (lo0, mx, cnt0, act0))
    has_tie = cnt_thr > kf
    any_tie = jnp.max(jnp.where(has_tie, 1.0, 0.0)) > 0.5

    def write_bias(keep):
        biasm_scr[...] = jnp.where(keep(sctm_scr[...], 0), 0.0, NEG_BIG).T

        def body(c, carry):
            bias_scr[c] = jnp.where(keep(sct_scr[c], MW + c * KC), 0.0, NEG_BIG).T
            return carry
        lax.fori_loop(0, n_seq, body, 0)

    @pl.when(jnp.logical_not(any_tie))
    def _():
        write_bias(lambda x, p0: x >= thr)

    @pl.when(any_tie)
    def _():
        far = 2 ** 30

        def tie_pos(x, p0):
            pos = p0 + lax.broadcasted_iota(jnp.int32, x.shape, 0)
            return jnp.where(x == thr, pos, far)

        need = kf - count(lambda x, p0: x > thr)

        def pos_step(i, last):
            cand = last | lax.shift_left(jnp.int32(1), 11 - i)
            below = count(lambda x, p0: tie_pos(x, p0) < cand)
            return jnp.where(below < need, cand, last)

        last = lax.fori_loop(0, 12, pos_step, jnp.zeros((1, QB), jnp.int32))
        last = jnp.where(has_tie, last, far - 1)
        write_bias(lambda x, p0: jnp.where(x > thr, 1.0,
                                           jnp.where(tie_pos(x, p0) <= last, 1.0, 0.0)) > 0.5)

    m_scr[...] = jnp.full(m_scr.shape, NEG_BIG, jnp.float32)
    l_scr[...] = jnp.zeros(l_scr.shape, jnp.float32)
    acc_scr[...] = jnp.zeros(acc_scr.shape, jnp.float32)

    def attend(bias, k_c, v_c):
        kw = k_c.shape[0]
        for g in range(N_KV_HEADS):
            sl = slice(g * LANES, (g + 1) * LANES)
            s = _dot_nt(qs_scr[g], k_c[:, sl])
            s = jnp.concatenate([s[r * QB:(r + 1) * QB] + bias for r in range(rep)], axis=0)
            m_prev = m_scr[g]
            m_new = jnp.maximum(m_prev, jnp.max(s, axis=-1, keepdims=True))
            alpha = jnp.exp2(m_prev - m_new)
            p = jnp.exp2(s - jnp.concatenate([m_new] * (kw // LANES), axis=1))
            l_scr[g] = alpha * l_scr[g] + jnp.sum(p, axis=-1, keepdims=True)
            acc_scr[g] = alpha * acc_scr[g] + _mm(p.astype(jnp.bfloat16), v_c[:, sl])
            m_scr[g] = m_new

    attend(biasm_scr[...], km_ref[...], vm_ref[...])

    def attend_chunk(c, carry):
        rows = pl.ds(pl.multiple_of(c * KC, KC), KC)
        attend(bias_scr[c], k_ref[0, rows, :], v_ref[0, rows, :])
        return carry

    lax.fori_loop(0, n_seq, attend_chunk, 0)

    for g in range(N_KV_HEADS):
        o = acc_scr[g] / l_scr[g]
        for r in range(rep):
            hd = g * rep + r
            o_ref[0, :, hd * LANES:(hd + 1) * LANES] = o[r * QB:(r + 1) * QB].astype(o_ref.dtype)


def _attention(q, qi, wi, k, v, ki, km, vm, kim, k_sel):
    b, s, _ = q.shape
    blk = lambda bi, qi_: (bi, qi_, 0)
    full = lambda bi, qi_: (bi, 0, 0)
    const = lambda bi, qi_: (0, 0)
    rep = N_HEADS // N_KV_HEADS
    f32 = jnp.float32
    return pl.pallas_call(
        functools.partial(_attn_kernel, k_sel),
        out_shape=jax.ShapeDtypeStruct((b, s, ATTN_WIDTH), jnp.bfloat16),
        grid=(b, s // QB),
        in_specs=[
            pl.BlockSpec((1, QB, ATTN_WIDTH), blk),
            pl.BlockSpec((1, QB, IDX_WIDTH), blk),
            pl.BlockSpec((1, QB, LANES), blk),
            pl.BlockSpec((1, s, KV_WIDTH), full),
            pl.BlockSpec((1, s, KV_WIDTH), full),
            pl.BlockSpec((1, s, LANES), full),
            pl.BlockSpec((MW, KV_WIDTH), const),
            pl.BlockSpec((MW, KV_WIDTH), const),
            pl.BlockSpec((MW, LANES), const),
        ],
        out_specs=pl.BlockSpec((1, QB, ATTN_WIDTH), blk),
        scratch_shapes=[
            pltpu.VMEM((N_KV_HEADS, rep * QB, LANES), jnp.bfloat16),
            pltpu.VMEM((N_IDX_HEADS // 4, 4 * QB, LANES), jnp.bfloat16),
            pltpu.VMEM((MW, QB), f32),
            pltpu.VMEM((s // KC, KC, QB), f32),
            pltpu.VMEM((QB, MW), f32),
            pltpu.VMEM((s // KC, QB, KC), f32),
            pltpu.VMEM((N_KV_HEADS, rep * QB, LANES), f32),
            pltpu.VMEM((N_KV_HEADS, rep * QB, LANES), f32),
            pltpu.VMEM((N_KV_HEADS, rep * QB, LANES), f32),
        ],
        compiler_params=_params(("parallel", "arbitrary")),
        name="dsa_attention",
    )(q, qi, wi, k, v, ki, km, vm, kim)


def _merge_kernel(tiles_per_seq, h_ref, attn_ref, xc_ref, gb_ref, cg_ref, ga0_ref, ga1_ref,
                  gc0_ref, gc1_ref, xc_prev_ref, cg_prev_ref, xc_meta_ref, cg_meta_ref,
                  cw_ref, cb_ref, wa_ref, wc_ref, wo_ref, o_ref):
    f32 = jnp.float32
    first = pl.program_id(0) % tiles_per_seq == 0
    gx = cg_ref[...].astype(f32) * xc_ref[...].astype(f32)
    halo_prev = cg_prev_ref[...].astype(f32) * xc_prev_ref[...].astype(f32)
    halo_meta = cg_meta_ref[...].astype(f32) * xc_meta_ref[...].astype(f32)
    halo = jnp.where(first, halo_meta, halo_prev)
    h1, h2 = halo[HALO - 1:HALO], halo[HALO - 2:HALO - 1]
    row = lax.broadcasted_iota(jnp.int32, gx.shape, 0)
    prev1 = jnp.where(row == 0, h1, pltpu.roll(gx, 1, 0))
    prev2 = jnp.where(row == 0, h2, jnp.where(row == 1, h1, pltpu.roll(gx, 2, 0)))
    cw = cw_ref[...]
    conv = cw[0:1] * prev2 + cw[1:2] * prev1 + cw[2:3] * gx + cb_ref[...]
    feat = (gb_ref[...].astype(f32) * conv).astype(jnp.bfloat16)
    y_conv = _mm(feat, wc_ref[...])
    y_attn = _mm(attn_ref[...], wa_ref[...])
    half = D_MODEL // 2
    sig = lambda r: jax.nn.sigmoid(r[...].astype(f32))
    merged = jnp.concatenate(
        [sig(ga0_ref) * y_attn[:, :half] + sig(gc0_ref) * y_conv[:, :half],
         sig(ga1_ref) * y_attn[:, half:] + sig(gc1_ref) * y_conv[:, half:]], axis=1)
    o_ref[...] = h_ref[...] + _mm(merged.astype(jnp.bfloat16), wo_ref[...])


def _merge(h, attn, zg, zg_meta, conv_w, conv_b, wa, wc, wo, seq):
    n = h.shape[0]
    tm = MERGE_TM
    tiles_per_seq = seq // tm
    row2 = lambda i: (i, 0)
    const = lambda i: (0, 0)
    col = lambda c: pl.BlockSpec((tm, CONV_WIDTH), lambda i: (i, c))
    prev = lambda c: pl.BlockSpec((HALO, CONV_WIDTH),
                                  lambda i: (jnp.maximum(i * (tm // HALO) - 1, 0), c))
    meta = lambda c: pl.BlockSpec((HALO, CONV_WIDTH), lambda i: (N_META // HALO - 1, c))
    single = pl.Buffered(1)
    return pl.pallas_call(
        functools.partial(_merge_kernel, tiles_per_seq),
        out_shape=jax.ShapeDtypeStruct((n, D_MODEL), jnp.float32),
        grid=(n // tm,),
        in_specs=[
            pl.BlockSpec((tm, D_MODEL), row2),
            pl.BlockSpec((tm, ATTN_WIDTH), row2),
            col(0), col(1), col(2), col(3), col(4), col(5), col(6),
            prev(0), prev(2), meta(0), meta(2),
            pl.BlockSpec((CONV_K, CONV_WIDTH), const),
            pl.BlockSpec((1, CONV_WIDTH), const),
            pl.BlockSpec((ATTN_WIDTH, D_MODEL), const, pipeline_mode=single),
            pl.BlockSpec((CONV_WIDTH, D_MODEL), const, pipeline_mode=single),
            pl.BlockSpec((D_MODEL, D_MODEL), const, pipeline_mode=single),
        ],
        out_specs=pl.BlockSpec((tm, D_MODEL), row2),
        compiler_params=_params(("parallel",)),
        name="merge",
    )(h, attn, zg, zg, zg, zg, zg, zg, zg, zg, zg, zg_meta, zg_meta, conv_w, conv_b, wa, wc, wo)


def _rope_tables(pos, head_dim):
    rot = head_dim // ROT_DIV
    half = rot // 2
    inv = ROPE_THETA ** (-jnp.arange(0, rot, 2, dtype=jnp.float32) / rot)
    ang = pos.astype(jnp.float32)[:, None] * inv[None, :]
    cos, sin = jnp.cos(ang), jnp.sin(ang)
    n = pos.shape[0]
    pad = head_dim - rot
    c = jnp.concatenate([cos, cos, jnp.ones((n, pad), jnp.float32)], axis=1)
    zeros_h = jnp.zeros((n, half), jnp.float32)
    zeros_p = jnp.zeros((n, pad), jnp.float32)
    s_lo = jnp.concatenate([-sin, zeros_h, zeros_p], axis=1)
    s_hi = jnp.concatenate([zeros_h, sin, zeros_p], axis=1)
    tile = LANES // head_dim
    return [jnp.tile(t, (1, tile)) for t in (c, s_lo, s_hi)]


def kernel(x, meta_tokens, ffn1_norm_g, ffn1_w_gate, ffn1_w_up, ffn1_w_down, mix_norm_g, w_in, q_norm_g, k_norm_g, conv_w, conv_b, w_attn_branch, w_conv_branch, w_out, ffn2_norm_g, ffn2_w_gate, ffn2_w_up, ffn2_w_down):
    bsz, seq, d = x.shape
    bf = jnp.bfloat16
    k_sel = min(TOPK_MAX, seq // 4)
    assert d == D_MODEL and seq % QB == 0 and k_sel <= KC

    widths = [ATTN_WIDTH, KV_WIDTH, KV_WIDTH, IDX_WIDTH, IDX_DIM, N_IDX_HEADS,
              CONV_WIDTH, CONV_WIDTH, CONV_WIDTH, D_MODEL, D_MODEL]
    offs = [int(o) for o in np.concatenate([[0], np.cumsum(widths)])]
    seg = lambda w, i, j=None: w[:, offs[i]:offs[i + 1 if j is None else j]].astype(bf)

    h = x.reshape(bsz * seq, d)
    hm = meta_tokens.astype(x.dtype)
    pos_seq = jnp.arange(N_META, N_META + seq, dtype=jnp.int32)
    pos_meta = jnp.arange(N_META, dtype=jnp.int32)
    tabs_seq = jnp.stack(_rope_tables(pos_seq, HEAD_DIM) + _rope_tables(pos_seq, IDX_DIM))
    tabs_meta = jnp.stack(_rope_tables(pos_meta, HEAD_DIM) + _rope_tables(pos_meta, IDX_DIM))

    for l in range(ffn1_norm_g.shape[0]):
        wl = w_in[l]
        w_attn = (seg(wl, 0), seg(wl, 3), seg(wl, 1), seg(wl, 2),
                  jnp.pad(seg(wl, 4, 6), ((0, 0), (0, LANES - IDX_DIM - N_IDX_HEADS))))
        w_g = seg(wl, 6, 11)
        g1 = ffn1_norm_g[l][None]
        gm = mix_norm_g[l][None]
        wg1, wu1, wd1 = ffn1_w_gate[l].astype(bf), ffn1_w_up[l].astype(bf), ffn1_w_down[l].astype(bf)

        h = _ffn(h, g1, wg1, wu1, wd1)
        hm = _ffn(hm, g1, wg1, wu1, wd1)

        gq, gk = q_norm_g[l][None], k_norm_g[l][None]
        q, qi, k, v, ki, wi = _proj_attn(h, gm, *w_attn, gq, gk, tabs_seq)
        _, _, km, vm, kim, _ = _proj_attn(hm, gm, *w_attn, gq, gk, tabs_meta)
        zg = _proj_gate(h, gm, w_g)
        zg_meta = _proj_gate(hm, gm, w_g)

        pad_rows = lambda a: jnp.pad(a, ((0, MW - N_META), (0, 0)))
        r3 = lambda a: a.reshape(bsz, seq, a.shape[-1])
        attn = _attention(r3(q), r3(qi), r3(wi), r3(k), r3(v), r3(ki),
                          pad_rows(km), pad_rows(vm), pad_rows(kim), k_sel)
        attn = attn.reshape(bsz * seq, ATTN_WIDTH)

        h = _merge(h, attn, zg, zg_meta, conv_w[l], conv_b[l][None],
                   w_attn_branch[l].astype(bf), w_conv_branch[l].astype(bf), w_out[l].astype(bf),
                   seq)

        h = _ffn(h, ffn2_norm_g[l][None], ffn2_w_gate[l].astype(bf), ffn2_w_up[l].astype(bf),
                 ffn2_w_down[l].astype(bf))

    return h.reshape(bsz, seq, d)
```

```python
import functools
import math

import jax
import jax.numpy as jnp
import numpy as np
from jax import lax
from jax.experimental import pallas as pl
from jax.experimental.pallas import tpu as pltpu

D_MODEL = 2048
N_META = 16
N_HEADS = 8
N_KV_HEADS = 2
HEAD_DIM = 128
ATTN_WIDTH = N_HEADS * HEAD_DIM
KV_WIDTH = N_KV_HEADS * HEAD_DIM
N_IDX_HEADS = 16
IDX_DIM = 64
IDX_WIDTH = N_IDX_HEADS * IDX_DIM
TOPK_MAX = 256
CONV_WIDTH = D_MODEL // 2
CONV_K = 3
D_FF = 5632
ROPE_THETA = 500000.0
ROT_DIV = 4
EPS = 1e-6

LANES = 128
VMEM_LIMIT = 56 * 1024 * 1024
NEG_BIG = -1e30
Q_SCALE = math.log2(math.e) / math.sqrt(HEAD_DIM)

FFN_TM, FFN_TF = 1024, 512
PROJ_TM = 512
GATE_TM, GATE_TN = 1024, 1024
MERGE_TM = 256
QB = 256
KC = 256
MW = 128
HALO = 16


def _params(sem):
    return pltpu.CompilerParams(dimension_semantics=sem, vmem_limit_bytes=VMEM_LIMIT)


def _rms(x, g):
    ms = jnp.mean(x * x, axis=-1, keepdims=True)
    return x * lax.rsqrt(ms + EPS) * g


def _mm(a, b):
    return jnp.dot(a, b, preferred_element_type=jnp.float32)


def _ffn_kernel(h_ref, g_ref, wg_ref, wu_ref, wd_ref, o_ref, u_scr):
    j = pl.program_id(1)

    @pl.when(j == 0)
    def _():
        h = h_ref[...]
        u_scr[...] = _rms(h, g_ref[...]).astype(jnp.bfloat16)
        o_ref[...] = h

    u = u_scr[...]
    gate = _mm(u, wg_ref[...])
    up = _mm(u, wu_ref[...])
    a = (gate * jax.nn.sigmoid(gate)) * (up * 0.5)
    o_ref[...] += _mm(a.astype(jnp.bfloat16), wd_ref[...])


def _ffn(h, g, wg, wu, wd):
    n = h.shape[0]
    tm = min(FFN_TM, n)
    return pl.pallas_call(
        _ffn_kernel,
        out_shape=jax.ShapeDtypeStruct((n, D_MODEL), jnp.float32),
        grid=(n // tm, D_FF // FFN_TF),
        in_specs=[
            pl.BlockSpec((tm, D_MODEL), lambda i, j: (i, 0)),
            pl.BlockSpec((1, D_MODEL), lambda i, j: (0, 0)),
            pl.BlockSpec((D_MODEL, FFN_TF), lambda i, j: (0, j)),
            pl.BlockSpec((D_MODEL, FFN_TF), lambda i, j: (0, j)),
            pl.BlockSpec((FFN_TF, D_MODEL), lambda i, j: (j, 0)),
        ],
        out_specs=pl.BlockSpec((tm, D_MODEL), lambda i, j: (i, 0)),
        scratch_shapes=[pltpu.VMEM((tm, D_MODEL), jnp.bfloat16)],
        compiler_params=_params(("parallel", "arbitrary")),
        name="ffn",
    )(h, g, wg, wu, wd)


def _rope(x, cos, s_lo, s_hi, half):
    n = x.shape[-1]
    return x * cos + pltpu.roll(x, n - half, 1) * s_lo + pltpu.roll(x, half, 1) * s_hi


def _proj_attn_kernel(h_ref, g_ref, wq_ref, wqi_ref, wk_ref, wv_ref, wkw_ref, gq_ref, gk_ref,
                      tab_ref, q_ref, qi_ref, k_ref, v_ref, ki_ref, wi_ref):
    bf = jnp.bfloat16
    u = _rms(h_ref[...], g_ref[...]).astype(bf)
    ca, sa_lo, sa_hi = tab_ref[0], tab_ref[1], tab_ref[2]
    ci, si_lo, si_hi = tab_ref[3], tab_ref[4], tab_ref[5]
    half_a = HEAD_DIM // ROT_DIV // 2
    half_i = IDX_DIM // ROT_DIV // 2
    gq = gq_ref[...]
    gk = gk_ref[...]
    z = _mm(u, wq_ref[...])
    for hd in range(N_HEADS):
        sl = slice(hd * LANES, (hd + 1) * LANES)
        q = _rope(_rms(z[:, sl], gq), ca, sa_lo, sa_hi, half_a)
        q_ref[:, sl] = (q * Q_SCALE).astype(bf)
    z = _mm(u, wqi_ref[...])
    for p in range(IDX_WIDTH // LANES):
        sl = slice(p * LANES, (p + 1) * LANES)
        qi_ref[:, sl] = _rope(z[:, sl], ci, si_lo, si_hi, half_i).astype(bf)
    z = _mm(u, wk_ref[...])
    for hd in range(N_KV_HEADS):
        sl = slice(hd * LANES, (hd + 1) * LANES)
        k_ref[:, sl] = _rope(_rms(z[:, sl], gk), ca, sa_lo, sa_hi, half_a).astype(bf)
    v_ref[...] = _mm(u, wv_ref[...]).astype(bf)
    x = _mm(u, wkw_ref[...])
    lane = lax.broadcasted_iota(jnp.int32, x.shape, 1)
    is_ki = lane < IDX_DIM
    roped = _rope(x, jnp.where(is_ki, ci, 1.0), jnp.where(is_ki, si_lo, 0.0),
                  jnp.where(is_ki, si_hi, 0.0), half_i)
    swapped = pltpu.roll(roped, IDX_DIM, 1)
    ki_ref[...] = jnp.where(is_ki, roped, swapped).astype(bf)
    wi_ref[...] = swapped


def _proj_attn(h, g, wq, wqi, wk, wv, wkw, gq, gk, tabs):
    n = h.shape[0]
    tm = min(PROJ_TM, n)
    nt = tabs.shape[1] // tm
    row = lambda i: (i, 0)
    const = lambda i: (0, 0)
    bf = jnp.bfloat16
    weight = lambda w: pl.BlockSpec(w.shape, const, pipeline_mode=pl.Buffered(1))
    return pl.pallas_call(
        _proj_attn_kernel,
        out_shape=(
            jax.ShapeDtypeStruct((n, ATTN_WIDTH), bf),
            jax.ShapeDtypeStruct((n, IDX_WIDTH), bf),
            jax.ShapeDtypeStruct((n, KV_WIDTH), bf),
            jax.ShapeDtypeStruct((n, KV_WIDTH), bf),
            jax.ShapeDtypeStruct((n, LANES), bf),
            jax.ShapeDtypeStruct((n, LANES), jnp.float32),
        ),
        grid=(n // tm,),
        in_specs=[
            pl.BlockSpec((tm, D_MODEL), row),
            pl.BlockSpec((1, D_MODEL), const),
            weight(wq), weight(wqi), weight(wk), weight(wv), weight(wkw),
            pl.BlockSpec((1, LANES), const),
            pl.BlockSpec((1, LANES), const),
            pl.BlockSpec((6, tm, LANES), lambda i: (0, i % nt, 0)),
        ],
        out_specs=(
            pl.BlockSpec((tm, ATTN_WIDTH), row),
            pl.BlockSpec((tm, IDX_WIDTH), row),
            pl.BlockSpec((tm, KV_WIDTH), row),
            pl.BlockSpec((tm, KV_WIDTH), row),
            pl.BlockSpec((tm, LANES), row),
            pl.BlockSpec((tm, LANES), row),
        ),
        compiler_params=_params(("parallel",)),
        name="proj_attn",
    )(h, g, wq, wqi, wk, wv, wkw, gq, gk, tabs)


def _proj_gate_kernel(h_ref, g_ref, w_ref, o_ref, u_scr):
    @pl.when(pl.program_id(1) == 0)
    def _():
        u_scr[...] = _rms(h_ref[...], g_ref[...]).astype(jnp.bfloat16)

    o_ref[...] = _mm(u_scr[...], w_ref[...]).astype(o_ref.dtype)


def _proj_gate(h, g, w):
    n = h.shape[0]
    tm = min(GATE_TM, n)
    cols = w.shape[1]
    return pl.pallas_call(
        _proj_gate_kernel,
        out_shape=jax.ShapeDtypeStruct((n, cols), jnp.bfloat16),
        grid=(n // tm, cols // GATE_TN),
        in_specs=[
            pl.BlockSpec((tm, D_MODEL), lambda i, j: (i, 0)),
            pl.BlockSpec((1, D_MODEL), lambda i, j: (0, 0)),
            pl.BlockSpec((D_MODEL, GATE_TN), lambda i, j: (0, j)),
        ],
        out_specs=pl.BlockSpec((tm, GATE_TN), lambda i, j: (i, j)),
        scratch_shapes=[pltpu.VMEM((tm, D_MODEL), jnp.bfloat16)],
        compiler_params=_params(("parallel", "arbitrary")),
        name="proj_gate",
    )(h, g, w)


def _dot_nt(a, b):
    return lax.dot_general(a, b, (((1,), (1,)), ((), ())), preferred_element_type=jnp.float32)


def _attn_kernel(k_sel, q_ref, qi_ref, wi_ref, k_ref, v_ref, ki_ref, km_ref, vm_ref, kim_ref,
                 o_ref, qs_scr, qis_scr, sctm_scr, sct_scr, biasm_scr, bias_scr,
                 m_scr, acc_scr):
    qb = pl.program_id(1)
    n_seq = qb + 1
    rep = N_HEADS // N_KV_HEADS
    hpg = 4
    lane_q = lax.broadcasted_iota(jnp.int32, (QB, LANES), 1)
    neg_inf = float("-inf")

    for g in range(N_KV_HEADS):
        for r in range(rep):
            hd = g * rep + r
            qs_scr[g, r * QB:(r + 1) * QB, :] = q_ref[0, :, hd * LANES:(hd + 1) * LANES]
    for i in range(N_IDX_HEADS // hpg):
        for r in range(hpg):
            hd = i * hpg + r
            pair = qi_ref[0, :, (hd // 2) * LANES:(hd // 2 + 1) * LANES]
            keep = (lane_q < IDX_DIM) if hd % 2 == 0 else (lane_q >= IDX_DIM)
            qis_scr[i, r * QB:(r + 1) * QB, :] = jnp.where(keep, pair, jnp.zeros_like(pair))

    w_all = wi_ref[0]

    def index_scores(ki_c):
        kw = ki_c.shape[0]
        score = jnp.zeros((QB, kw), jnp.float32)
        for i in range(N_IDX_HEADS // hpg):
            d = _dot_nt(qis_scr[i], ki_c)
            for r in range(hpg):
                hd = i * hpg + r
                score = score + jnp.maximum(d[r * QB:(r + 1) * QB], 0.0) * w_all[:, hd:hd + 1]
        return score

    lane_m = lax.broadcasted_iota(jnp.int32, (QB, MW), 1)
    sctm_scr[...] = jnp.where(lane_m < N_META, index_scores(kim_ref[...]), neg_inf).T
    row = lax.broadcasted_iota(jnp.int32, (QB, KC), 0)
    lane = lax.broadcasted_iota(jnp.int32, (QB, KC), 1)

    def score_chunk(c, carry):
        ki_c = ki_ref[0, pl.ds(pl.multiple_of(c * KC, KC), KC), :]
        causal = (c - qb) * KC + lane <= row
        sct_scr[c] = jnp.where(causal, index_scores(ki_c), neg_inf).T
        return carry

    lax.fori_loop(0, n_seq, score_chunk, 0)

    def over_keys(f, init):
        acc = f(sctm_scr[...], 0, init)
        return lax.fori_loop(0, n_seq, lambda c, a: f(sct_scr[c], MW + c * KC, a), acc)

    def fold8(x):
        return jnp.sum(x.reshape(x.shape[0] // 8, 8, QB), axis=0)

    def count(pred):
        part = over_keys(lambda x, p0, a: a + fold8(jnp.where(pred(x, p0), 1.0, 0.0)),
                         jnp.zeros((8, QB), jnp.float32))
        return jnp.sum(part, axis=0, keepdims=True)

    kf = float(k_sel)
    qlane = lax.broadcasted_iota(jnp.int32, (1, QB), 1)
    n_valid = (N_META + 1 + qb * QB + qlane).astype(jnp.float32)

    def minmax(x, p0, carry):
        mn, mx = carry
        x3 = x.reshape(x.shape[0] // 8, 8, QB)
        mx = jnp.maximum(mx, jnp.max(x3, axis=0))
        mn = jnp.minimum(mn, jnp.min(jnp.where(x3 == neg_inf, float("inf"), x3), axis=0))
        return mn, mx

    mn, mx = over_keys(minmax, (jnp.full((8, QB), float("inf"), jnp.float32),
                                jnp.full((8, QB), neg_inf, jnp.float32)))
    mn = jnp.min(mn, axis=0, keepdims=True)
    mx = jnp.max(mx, axis=0, keepdims=True)

    def count_ge(t):
        return count(lambda x, p0: x >= t)

    def midpoint(lo, hi):
        return lo + (hi - lo) * 0.5

    def is_active(lo, hi, cnt_lo):
        mid = midpoint(lo, hi)
        return jnp.where((cnt_lo != kf) & (mid > lo) & (mid < hi), 1.0, 0.0)

    cnt_mx = count_ge(mx)
    at_max = cnt_mx >= kf
    lo0 = jnp.where(at_max, mx, mn)
    cnt0 = jnp.where(at_max, cnt_mx, n_valid)
    act0 = jnp.where(n_valid > kf, is_active(lo0, mx, cnt0), 0.0)

    def bisect_once(state):
        lo, hi, cnt_lo, act = state
        mid = midpoint(lo, hi)
        cnt = count_ge(mid)
        up = (act > 0.5) & (cnt >= kf)
        down = (act > 0.5) & (cnt < kf)
        lo = jnp.where(up, mid, lo)
        cnt_lo = jnp.where(up, cnt, cnt_lo)
        hi = jnp.where(down, mid, hi)
        return lo, hi, cnt_lo, act * is_active(lo, hi, cnt_lo)

    thr, _, cnt_thr, _ = lax.while_loop(lambda st: jnp.max(st[3]) > 0.5,
                                        lambda st: bisect_once(bisect_once(st)),
                                        (lo0, mx, cnt0, act0))
    has_tie = cnt_thr > kf
    any_tie = jnp.max(jnp.where(has_tie, 1.0, 0.0)) > 0.5

    def write_bias(keep):
        biasm_scr[...] = jnp.where(keep(sctm_scr[...], 0), 0.0, NEG_BIG).T

        def body(c, carry):
            bias_scr[c] = jnp.where(keep(sct_scr[c], MW + c * KC), 0.0, NEG_BIG).T
            return carry
        lax.fori_loop(0, n_seq, body, 0)

    @pl.when(jnp.logical_not(any_tie))
    def _():
        write_bias(lambda x, p0: x >= thr)

    @pl.when(any_tie)
    def _():
        far = 2 ** 30

        def tie_pos(x, p0):
            pos = p0 + lax.broadcasted_iota(jnp.int32, x.shape, 0)
            return jnp.where(x == thr, pos, far)

        need = kf - count(lambda x, p0: x > thr)

        def pos_step(i, last):
            cand = last | lax.shift_left(jnp.int32(1), 11 - i)
            below = count(lambda x, p0: tie_pos(x, p0) < cand)
            return jnp.where(below < need, cand, last)

        last = lax.fori_loop(0, 12, pos_step, jnp.zeros((1, QB), jnp.int32))
        last = jnp.where(has_tie, last, far - 1)
        write_bias(lambda x, p0: jnp.where(x > thr, 1.0,
                                           jnp.where(tie_pos(x, p0) <= last, 1.0, 0.0)) > 0.5)

    m_scr[...] = jnp.full(m_scr.shape, NEG_BIG, jnp.float32)
    acc_scr[...] = jnp.zeros(acc_scr.shape, jnp.float32)

    def attend(bias, k_c, v_c):
        kw = k_c.shape[0]
        ones = jnp.ones((kw, LANES), jnp.bfloat16)
        for g in range(N_KV_HEADS):
            sl = slice(g * LANES, (g + 1) * LANES)
            s = _dot_nt(qs_scr[g], k_c[:, sl])
            s = jnp.concatenate([s[r * QB:(r + 1) * QB] + bias for r in range(rep)], axis=0)
            m_prev = m_scr[g]
            m_new = jnp.maximum(m_prev, jnp.max(s, axis=-1, keepdims=True))
            alpha = jnp.exp2(m_prev - m_new)
            shifted = s - jnp.concatenate([m_new] * (kw // LANES), axis=1)
            p = jnp.exp2(shifted.astype(jnp.bfloat16))
            pv = _mm(p, jnp.concatenate([v_c[:, sl], ones], axis=1))
            acc_scr[g] = jnp.concatenate([alpha, alpha], axis=1) * acc_scr[g] + pv
            m_scr[g] = m_new

    attend(biasm_scr[...], km_ref[...], vm_ref[...])

    def attend_chunk(c, carry):
        rows = pl.ds(pl.multiple_of(c * KC, KC), KC)
        attend(bias_scr[c], k_ref[0, rows, :], v_ref[0, rows, :])
        return carry

    lax.fori_loop(0, n_seq, attend_chunk, 0)

    for g in range(N_KV_HEADS):
        o = acc_scr[g, :, :LANES] / acc_scr[g, :, LANES:]
        for r in range(rep):
            hd = g * rep + r
            o_ref[0, :, hd * LANES:(hd + 1) * LANES] = o[r * QB:(r + 1) * QB].astype(o_ref.dtype)


def _attention(q, qi, wi, k, v, ki, km, vm, kim, k_sel):
    b, s, _ = q.shape
    blk = lambda bi, qi_: (bi, qi_, 0)
    full = lambda bi, qi_: (bi, 0, 0)
    const = lambda bi, qi_: (0, 0)
    rep = N_HEADS // N_KV_HEADS
    f32 = jnp.float32
    return pl.pallas_call(
        functools.partial(_attn_kernel, k_sel),
        out_shape=jax.ShapeDtypeStruct((b, s, ATTN_WIDTH), jnp.bfloat16),
        grid=(b, s // QB),
        in_specs=[
            pl.BlockSpec((1, QB, ATTN_WIDTH), blk),
            pl.BlockSpec((1, QB, IDX_WIDTH), blk),
            pl.BlockSpec((1, QB, LANES), blk),
            pl.BlockSpec((1, s, KV_WIDTH), full),
            pl.BlockSpec((1, s, KV_WIDTH), full),
            pl.BlockSpec((1, s, LANES), full),
            pl.BlockSpec((MW, KV_WIDTH), const),
            pl.BlockSpec((MW, KV_WIDTH), const),
            pl.BlockSpec((MW, LANES), const),
        ],
        out_specs=pl.BlockSpec((1, QB, ATTN_WIDTH), blk),
        scratch_shapes=[
            pltpu.VMEM((N_KV_HEADS, rep * QB, LANES), jnp.bfloat16),
            pltpu.VMEM((N_IDX_HEADS // 4, 4 * QB, LANES), jnp.bfloat16),
            pltpu.VMEM((MW, QB), f32),
            pltpu.VMEM((s // KC, KC, QB), f32),
            pltpu.VMEM((QB, MW), f32),
            pltpu.VMEM((s // KC, QB, KC), f32),
            pltpu.VMEM((N_KV_HEADS, rep * QB, LANES), f32),
            pltpu.VMEM((N_KV_HEADS, rep * QB, 2 * LANES), f32),
        ],
        compiler_params=_params(("parallel", "arbitrary")),
        name="dsa_attention",
    )(q, qi, wi, k, v, ki, km, vm, kim)


def _merge_kernel(tiles_per_seq, h_ref, attn_ref, xc_ref, gb_ref, cg_ref, ga0_ref, ga1_ref,
                  gc0_ref, gc1_ref, xc_prev_ref, cg_prev_ref, xc_meta_ref, cg_meta_ref,
                  cw_ref, cb_ref, wa_ref, wc_ref, wo_ref, o_ref):
    f32 = jnp.float32
    first = pl.program_id(0) % tiles_per_seq == 0
    gx = cg_ref[...].astype(f32) * xc_ref[...].astype(f32)
    halo_prev = cg_prev_ref[...].astype(f32) * xc_prev_ref[...].astype(f32)
    halo_meta = cg_meta_ref[...].astype(f32) * xc_meta_ref[...].astype(f32)
    halo = jnp.where(first, halo_meta, halo_prev)
    h1, h2 = halo[HALO - 1:HALO], halo[HALO - 2:HALO - 1]
    row = lax.broadcasted_iota(jnp.int32, gx.shape, 0)
    prev1 = jnp.where(row == 0, h1, pltpu.roll(gx, 1, 0))
    prev2 = jnp.where(row == 0, h2, jnp.where(row == 1, h1, pltpu.roll(gx, 2, 0)))
    cw = cw_ref[...]
    conv = cw[0:1] * prev2 + cw[1:2] * prev1 + cw[2:3] * gx + cb_ref[...]
    feat = (gb_ref[...].astype(f32) * conv).astype(jnp.bfloat16)
    y_conv = _mm(feat, wc_ref[...])
    y_attn = _mm(attn_ref[...], wa_ref[...])
    half = D_MODEL // 2
    sig = lambda r: jax.nn.sigmoid(r[...].astype(f32))
    merged = jnp.concatenate(
        [sig(ga0_ref) * y_attn[:, :half] + sig(gc0_ref) * y_conv[:, :half],
         sig(ga1_ref) * y_attn[:, half:] + sig(gc1_ref) * y_conv[:, half:]], axis=1)
    o_ref[...] = h_ref[...] + _mm(merged.astype(jnp.bfloat16), wo_ref[...])


def _merge(h, attn, zg, zg_meta, conv_w, conv_b, wa, wc, wo, seq):
    n = h.shape[0]
    tm = MERGE_TM
    tiles_per_seq = seq // tm
    row2 = lambda i: (i, 0)
    const = lambda i: (0, 0)
    col = lambda c: pl.BlockSpec((tm, CONV_WIDTH), lambda i: (i, c))
    prev = lambda c: pl.BlockSpec((HALO, CONV_WIDTH),
                                  lambda i: (jnp.maximum(i * (tm // HALO) - 1, 0), c))
    meta = lambda c: pl.BlockSpec((HALO, CONV_WIDTH), lambda i: (N_META // HALO - 1, c))
    single = pl.Buffered(1)
    return pl.pallas_call(
        functools.partial(_merge_kernel, tiles_per_seq),
        out_shape=jax.ShapeDtypeStruct((n, D_MODEL), jnp.float32),
        grid=(n // tm,),
        in_specs=[
            pl.BlockSpec((tm, D_MODEL), row2),
            pl.BlockSpec((tm, ATTN_WIDTH), row2),
            col(0), col(1), col(2), col(3), col(4), col(5), col(6),
            prev(0), prev(2), meta(0), meta(2),
            pl.BlockSpec((CONV_K, CONV_WIDTH), const),
            pl.BlockSpec((1, CONV_WIDTH), const),
            pl.BlockSpec((ATTN_WIDTH, D_MODEL), const, pipeline_mode=single),
            pl.BlockSpec((CONV_WIDTH, D_MODEL), const, pipeline_mode=single),
            pl.BlockSpec((D_MODEL, D_MODEL), const, pipeline_mode=single),
        ],
        out_specs=pl.BlockSpec((tm, D_MODEL), row2),
        compiler_params=_params(("parallel",)),
        name="merge",
    )(h, attn, zg, zg, zg, zg, zg, zg, zg, zg, zg, zg_meta, zg_meta, conv_w, conv_b, wa, wc, wo)


def _rope_tables(pos, head_dim):
    rot = head_dim // ROT_DIV
    half = rot // 2
    inv = ROPE_THETA ** (-jnp.arange(0, rot, 2, dtype=jnp.float32) / rot)
    ang = pos.astype(jnp.float32)[:, None] * inv[None, :]
    cos, sin = jnp.cos(ang), jnp.sin(ang)
    n = pos.shape[0]
    pad = head_dim - rot
    c = jnp.concatenate([cos, cos, jnp.ones((n, pad), jnp.float32)], axis=1)
    zeros_h = jnp.zeros((n, half), jnp.float32)
    zeros_p = jnp.zeros((n, pad), jnp.float32)
    s_lo = jnp.concatenate([-sin, zeros_h, zeros_p], axis=1)
    s_hi = jnp.concatenate([zeros_h, sin, zeros_p], axis=1)
    tile = LANES // head_dim
    return [jnp.tile(t, (1, tile)) for t in (c, s_lo, s_hi)]


def kernel(x, meta_tokens, ffn1_norm_g, ffn1_w_gate, ffn1_w_up, ffn1_w_down, mix_norm_g, w_in, q_norm_g, k_norm_g, conv_w, conv_b, w_attn_branch, w_conv_branch, w_out, ffn2_norm_g, ffn2_w_gate, ffn2_w_up, ffn2_w_down):
    bsz, seq, d = x.shape
    bf = jnp.bfloat16
    k_sel = min(TOPK_MAX, seq // 4)
    assert d == D_MODEL and seq % QB == 0 and k_sel <= KC

    widths = [ATTN_WIDTH, KV_WIDTH, KV_WIDTH, IDX_WIDTH, IDX_DIM, N_IDX_HEADS,
              CONV_WIDTH, CONV_WIDTH, CONV_WIDTH, D_MODEL, D_MODEL]
    offs = [int(o) for o in np.concatenate([[0], np.cumsum(widths)])]
    seg = lambda w, i, j=None: w[:, offs[i]:offs[i + 1 if j is None else j]].astype(bf)

    h = x.reshape(bsz * seq, d)
    hm = meta_tokens.astype(x.dtype)
    pos_seq = jnp.arange(N_META, N_META + seq, dtype=jnp.int32)
    pos_meta = jnp.arange(N_META, dtype=jnp.int32)
    tabs_seq = jnp.stack(_rope_tables(pos_seq, HEAD_DIM) + _rope_tables(pos_seq, IDX_DIM))
    tabs_meta = jnp.stack(_rope_tables(pos_meta, HEAD_DIM) + _rope_tables(pos_meta, IDX_DIM))

    for l in range(ffn1_norm_g.shape[0]):
        wl = w_in[l]
        w_attn = (seg(wl, 0), seg(wl, 3), seg(wl, 1), seg(wl, 2),
                  jnp.pad(seg(wl, 4, 6), ((0, 0), (0, LANES - IDX_DIM - N_IDX_HEADS))))
        w_g = seg(wl, 6, 11)
        g1 = ffn1_norm_g[l][None]
        gm = mix_norm_g[l][None]
        wg1, wu1, wd1 = ffn1_w_gate[l].astype(bf), ffn1_w_up[l].astype(bf), ffn1_w_down[l].astype(bf)

        h = _ffn(h, g1, wg1, wu1, wd1)
        hm = _ffn(hm, g1, wg1, wu1, wd1)

        gq, gk = q_norm_g[l][None], k_norm_g[l][None]
        q, qi, k, v, ki, wi = _proj_attn(h, gm, *w_attn, gq, gk, tabs_seq)
        _, _, km, vm, kim, _ = _proj_attn(hm, gm, *w_attn, gq, gk, tabs_meta)
        zg = _proj_gate(h, gm, w_g)
        zg_meta = _proj_gate(hm, gm, w_g)

        pad_rows = lambda a: jnp.pad(a, ((0, MW - N_META), (0, 0)))
        r3 = lambda a: a.reshape(bsz, seq, a.shape[-1])
        attn = _attention(r3(q), r3(qi), r3(wi), r3(k), r3(v), r3(ki),
                          pad_rows(km), pad_rows(vm), pad_rows(kim), k_sel)
        attn = attn.reshape(bsz * seq, ATTN_WIDTH)

        h = _merge(h, attn, zg, zg_meta, conv_w[l], conv_b[l][None],
                   w_attn_branch[l].astype(bf), w_conv_branch[l].astype(bf), w_out[l].astype(bf),
                   seq)

        h = _ffn(h, ffn2_norm_g[l][None], ffn2_w_gate[l].astype(bf), ffn2_w_up[l].astype(bf),
                 ffn2_w_down[l].astype(bf))

    return h.reshape(bsz, seq, d)
```

```python
import functools
import math

import jax
import jax.numpy as jnp
import numpy as np
from jax import lax
from jax.experimental import pallas as pl
from jax.experimental.pallas import tpu as pltpu

D_MODEL = 2048
N_META = 16
N_HEADS = 8
N_KV_HEADS = 2
HEAD_DIM = 128
ATTN_WIDTH = N_HEADS * HEAD_DIM
KV_WIDTH = N_KV_HEADS * HEAD_DIM
N_IDX_HEADS = 16
IDX_DIM = 64
IDX_WIDTH = N_IDX_HEADS * IDX_DIM
TOPK_MAX = 256
CONV_WIDTH = D_MODEL // 2
CONV_K = 3
D_FF = 5632
ROPE_THETA = 500000.0
ROT_DIV = 4
EPS = 1e-6

LANES = 128
VMEM_LIMIT = 56 * 1024 * 1024
NEG_BIG = -1e30
Q_SCALE = math.log2(math.e) / math.sqrt(HEAD_DIM)

FFN_TM, FFN_TF = 1024, 512
PROJ_TM = 256
GATE_TM, GATE_TN = 1024, 1024
MERGE_TM = 256
QB = 256
KC = 256
MW = 128
BISECT_STEPS = 3
HALO = 16


def _params(sem):
    return pltpu.CompilerParams(dimension_semantics=sem, vmem_limit_bytes=VMEM_LIMIT)


def _rms(x, g):
    ms = jnp.mean(x * x, axis=-1, keepdims=True)
    return x * lax.rsqrt(ms + EPS) * g


def _mm(a, b):
    return jnp.dot(a, b, preferred_element_type=jnp.float32)


def _ffn_kernel(h_ref, g_ref, wg_ref, wu_ref, wd_ref, o_ref, u_scr):
    j = pl.program_id(1)

    @pl.when(j == 0)
    def _():
        h = h_ref[...]
        u_scr[...] = _rms(h, g_ref[...]).astype(jnp.bfloat16)
        o_ref[...] = h

    u = u_scr[...]
    gate = _mm(u, wg_ref[...])
    up = _mm(u, wu_ref[...])
    a = (gate * jax.nn.sigmoid(gate)) * (up * 0.5)
    o_ref[...] += _mm(a.astype(jnp.bfloat16), wd_ref[...])


def _ffn(h, g, wg, wu, wd):
    n = h.shape[0]
    tm = min(FFN_TM, n)
    return pl.pallas_call(
        _ffn_kernel,
        out_shape=jax.ShapeDtypeStruct((n, D_MODEL), jnp.float32),
        grid=(n // tm, D_FF // FFN_TF),
        in_specs=[
            pl.BlockSpec((tm, D_MODEL), lambda i, j: (i, 0)),
            pl.BlockSpec((1, D_MODEL), lambda i, j: (0, 0)),
            pl.BlockSpec((D_MODEL, FFN_TF), lambda i, j: (0, j)),
            pl.BlockSpec((D_MODEL, FFN_TF), lambda i, j: (0, j)),
            pl.BlockSpec((FFN_TF, D_MODEL), lambda i, j: (j, 0)),
        ],
        out_specs=pl.BlockSpec((tm, D_MODEL), lambda i, j: (i, 0)),
        scratch_shapes=[pltpu.VMEM((tm, D_MODEL), jnp.bfloat16)],
        compiler_params=_params(("parallel", "arbitrary")),
        name="ffn",
    )(h, g, wg, wu, wd)


def _rope(x, cos, s_lo, s_hi, half):
    n = x.shape[-1]
    return x * cos + pltpu.roll(x, n - half, 1) * s_lo + pltpu.roll(x, half, 1) * s_hi


def _proj_attn_kernel(h_ref, g_ref, wq_ref, wqi_ref, wk_ref, wv_ref, wkw_ref, gq_ref, gk_ref,
                      tab_ref, q_ref, qi_ref, k_ref, v_ref, ki_ref, wi_ref):
    bf = jnp.bfloat16
    u = _rms(h_ref[...], g_ref[...]).astype(bf)
    ca, sa_lo, sa_hi = tab_ref[0], tab_ref[1], tab_ref[2]
    ci, si_lo, si_hi = tab_ref[3], tab_ref[4], tab_ref[5]
    half_a = HEAD_DIM // ROT_DIV // 2
    half_i = IDX_DIM // ROT_DIV // 2
    gq = gq_ref[...]
    gk = gk_ref[...]
    z = _mm(u, wq_ref[...])
    for hd in range(N_HEADS):
        sl = slice(hd * LANES, (hd + 1) * LANES)
        q = _rope(_rms(z[:, sl], gq), ca, sa_lo, sa_hi, half_a)
        q_ref[:, sl] = (q * Q_SCALE).astype(bf)
    z = _mm(u, wqi_ref[...])
    for p in range(IDX_WIDTH // LANES):
        sl = slice(p * LANES, (p + 1) * LANES)
        qi_ref[:, sl] = _rope(z[:, sl], ci, si_lo, si_hi, half_i).astype(bf)
    z = _mm(u, wk_ref[...])
    for hd in range(N_KV_HEADS):
        sl = slice(hd * LANES, (hd + 1) * LANES)
        k_ref[:, sl] = _rope(_rms(z[:, sl], gk), ca, sa_lo, sa_hi, half_a).astype(bf)
    v_ref[...] = _mm(u, wv_ref[...]).astype(bf)
    x = _mm(u, wkw_ref[...])
    lane = lax.broadcasted_iota(jnp.int32, x.shape, 1)
    is_ki = lane < IDX_DIM
    roped = _rope(x, jnp.where(is_ki, ci, 1.0), jnp.where(is_ki, si_lo, 0.0),
                  jnp.where(is_ki, si_hi, 0.0), half_i)
    swapped = pltpu.roll(roped, IDX_DIM, 1)
    ki_ref[...] = jnp.where(is_ki, roped, swapped).astype(bf)
    wi_ref[...] = swapped


def _proj_attn(h, g, wq, wqi, wk, wv, wkw, gq, gk, tabs):
    n = h.shape[0]
    tm = min(PROJ_TM, n)
    nt = tabs.shape[1] // tm
    row = lambda i: (i, 0)
    const = lambda i: (0, 0)
    bf = jnp.bfloat16
    weight = lambda w: pl.BlockSpec(w.shape, const)
    return pl.pallas_call(
        _proj_attn_kernel,
        out_shape=(
            jax.ShapeDtypeStruct((n, ATTN_WIDTH), bf),
            jax.ShapeDtypeStruct((n, IDX_WIDTH), bf),
            jax.ShapeDtypeStruct((n, KV_WIDTH), bf),
            jax.ShapeDtypeStruct((n, KV_WIDTH), bf),
            jax.ShapeDtypeStruct((n, LANES), bf),
            jax.ShapeDtypeStruct((n, LANES), jnp.float32),
        ),
        grid=(n // tm,),
        in_specs=[
            pl.BlockSpec((tm, D_MODEL), row),
            pl.BlockSpec((1, D_MODEL), const),
            weight(wq), weight(wqi), weight(wk), weight(wv), weight(wkw),
            pl.BlockSpec((1, LANES), const),
            pl.BlockSpec((1, LANES), const),
            pl.BlockSpec((6, tm, LANES), lambda i: (0, i % nt, 0)),
        ],
        out_specs=(
            pl.BlockSpec((tm, ATTN_WIDTH), row),
            pl.BlockSpec((tm, IDX_WIDTH), row),
            pl.BlockSpec((tm, KV_WIDTH), row),
            pl.BlockSpec((tm, KV_WIDTH), row),
            pl.BlockSpec((tm, LANES), row),
            pl.BlockSpec((tm, LANES), row),
        ),
        compiler_params=_params(("parallel",)),
        name="proj_attn",
    )(h, g, wq, wqi, wk, wv, wkw, gq, gk, tabs)


def _proj_gate_kernel(h_ref, g_ref, w_ref, o_ref, u_scr):
    @pl.when(pl.program_id(1) == 0)
    def _():
        u_scr[...] = _rms(h_ref[...], g_ref[...]).astype(jnp.bfloat16)

    o_ref[...] = _mm(u_scr[...], w_ref[...]).astype(o_ref.dtype)


def _proj_gate(h, g, w):
    n = h.shape[0]
    tm = min(GATE_TM, n)
    cols = w.shape[1]
    return pl.pallas_call(
        _proj_gate_kernel,
        out_shape=jax.ShapeDtypeStruct((n, cols), jnp.bfloat16),
        grid=(n // tm, cols // GATE_TN),
        in_specs=[
            pl.BlockSpec((tm, D_MODEL), lambda i, j: (i, 0)),
            pl.BlockSpec((1, D_MODEL), lambda i, j: (0, 0)),
            pl.BlockSpec((D_MODEL, GATE_TN), lambda i, j: (0, j)),
        ],
        out_specs=pl.BlockSpec((tm, GATE_TN), lambda i, j: (i, j)),
        scratch_shapes=[pltpu.VMEM((tm, D_MODEL), jnp.bfloat16)],
        compiler_params=_params(("parallel", "arbitrary")),
        name="proj_gate",
    )(h, g, w)


def _dot_nt(a, b):
    return lax.dot_general(a, b, (((1,), (1,)), ((), ())), preferred_element_type=jnp.float32)


def _attn_kernel(k_sel, n_cast, q_ref, qi_ref, wi_ref, k_ref, v_ref, ki_ref, km_ref, vm_ref,
                 kim_ref, *rest):
    cast_in, o_ref, cast_out = rest[:n_cast], rest[n_cast], rest[n_cast + 1:2 * n_cast + 1]
    (qs_scr, qis_scr, sctm_scr, sct_scr, biasm_scr, bias_scr,
     m_scr, l_scr, acc_scr) = rest[2 * n_cast + 1:]
    for src, dst in zip(cast_in, cast_out):
        dst[...] = src[...].astype(dst.dtype)

    qb = pl.program_id(1)
    n_seq = qb + 1
    n_pairs = (n_seq + 1) // 2
    rep = N_HEADS // N_KV_HEADS
    hpg = 4
    lane_q = lax.broadcasted_iota(jnp.int32, (QB, LANES), 1)
    neg_inf = float("-inf")

    for g in range(N_KV_HEADS):
        for r in range(rep):
            hd = g * rep + r
            qs_scr[g, r * QB:(r + 1) * QB, :] = q_ref[0, :, hd * LANES:(hd + 1) * LANES]
    for i in range(N_IDX_HEADS // hpg):
        for r in range(hpg):
            hd = i * hpg + r
            pair = qi_ref[0, :, (hd // 2) * LANES:(hd // 2 + 1) * LANES]
            keep = (lane_q < IDX_DIM) if hd % 2 == 0 else (lane_q >= IDX_DIM)
            qis_scr[i, r * QB:(r + 1) * QB, :] = jnp.where(keep, pair, jnp.zeros_like(pair))

    w_all = wi_ref[0]

    def index_scores(ki_c):
        kw = ki_c.shape[0]
        score = jnp.zeros((QB, kw), jnp.float32)
        for i in range(N_IDX_HEADS // hpg):
            d = _dot_nt(qis_scr[i], ki_c)
            for r in range(hpg):
                hd = i * hpg + r
                score = score + jnp.maximum(d[r * QB:(r + 1) * QB], 0.0) * w_all[:, hd:hd + 1]
        return score

    lane_m = lax.broadcasted_iota(jnp.int32, (QB, MW), 1)
    sctm_scr[...] = jnp.where(lane_m < N_META, index_scores(kim_ref[...]), neg_inf).T
    row = lax.broadcasted_iota(jnp.int32, (QB, KC), 0)
    lane = lax.broadcasted_iota(jnp.int32, (QB, KC), 1)

    def score_chunk(c, carry):
        ki_c = ki_ref[0, pl.ds(pl.multiple_of(c * KC, KC), KC), :]
        causal = (c - qb) * KC + lane <= row
        sct_scr[c] = jnp.where(causal, index_scores(ki_c), neg_inf).T
        return carry

    lax.fori_loop(0, n_seq, score_chunk, 0)

    @pl.when(n_seq % 2 == 1)
    def _():
        sct_scr[n_seq] = jnp.full((KC, QB), neg_inf, jnp.float32)

    def over_keys(f, init):
        def pair(i, a):
            c = 2 * i
            a = f(sct_scr[c], MW + c * KC, a)
            return f(sct_scr[c + 1], MW + (c + 1) * KC, a)
        return lax.fori_loop(0, n_pairs, pair, f(sctm_scr[...], 0, init))

    def fold8(x):
        return jnp.sum(x.reshape(x.shape[0] // 8, 8, QB), axis=0)

    def count(pred):
        part = over_keys(lambda x, p0, a: a + fold8(jnp.where(pred(x, p0), 1.0, 0.0)),
                         jnp.zeros((8, QB), jnp.float32))
        return jnp.sum(part, axis=0, keepdims=True)

    kf = float(k_sel)
    qlane = lax.broadcasted_iota(jnp.int32, (1, QB), 1)
    n_valid = (N_META + 1 + qb * QB + qlane).astype(jnp.float32)

    def minmax(x, p0, carry):
        mn, mx = carry
        x3 = x.reshape(x.shape[0] // 8, 8, QB)
        mx = jnp.maximum(mx, jnp.max(x3, axis=0))
        mn = jnp.minimum(mn, jnp.min(jnp.where(x3 == neg_inf, float("inf"), x3), axis=0))
        return mn, mx

    mn, mx = over_keys(minmax, (jnp.full((8, QB), float("inf"), jnp.float32),
                                jnp.full((8, QB), neg_inf, jnp.float32)))
    mn = jnp.min(mn, axis=0, keepdims=True)
    mx = jnp.max(mx, axis=0, keepdims=True)

    def count_ge(t):
        return count(lambda x, p0: x >= t)

    def midpoint(lo, hi):
        return lo + (hi - lo) * 0.5

    def is_active(lo, hi, cnt_lo):
        mid = midpoint(lo, hi)
        return jnp.where((cnt_lo != kf) & (mid > lo) & (mid < hi), 1.0, 0.0)

    cnt_mx = count_ge(mx)
    at_max = cnt_mx >= kf
    lo0 = jnp.where(at_max, mx, mn)
    cnt0 = jnp.where(at_max, cnt_mx, n_valid)
    act0 = jnp.where(n_valid > kf, is_active(lo0, mx, cnt0), 0.0)

    def bisect_once(state):
        lo, hi, cnt_lo, act = state
        mid = midpoint(lo, hi)
        cnt = count_ge(mid)
        up = (act > 0.5) & (cnt >= kf)
        down = (act > 0.5) & (cnt < kf)
        lo = jnp.where(up, mid, lo)
        cnt_lo = jnp.where(up, cnt, cnt_lo)
        hi = jnp.where(down, mid, hi)
        return lo, hi, cnt_lo, act * is_active(lo, hi, cnt_lo)

    def bisect_trip(st):
        for _ in range(BISECT_STEPS):
            st = bisect_once(st)
        return st

    thr, _, cnt_thr, _ = lax.while_loop(lambda st: jnp.max(st[3]) > 0.5, bisect_trip,
                                        (lo0, mx, cnt0, act0))
    has_tie = cnt_thr > kf
    any_tie = jnp.max(jnp.where(has_tie, 1.0, 0.0)) > 0.5

    def write_bias(keep):
        biasm_scr[...] = jnp.where(keep(sctm_scr[...], 0), 0.0, NEG_BIG).T

        def body(c, carry):
            bias_scr[c] = jnp.where(keep(sct_scr[c], MW + c * KC), 0.0, NEG_BIG).T
            return carry
        lax.fori_loop(0, n_seq, body, 0)

    @pl.when(jnp.logical_not(any_tie))
    def _():
        write_bias(lambda x, p0: x >= thr)

    @pl.when(any_tie)
    def _():
        far = 2 ** 30

        def tie_pos(x, p0):
            pos = p0 + lax.broadcasted_iota(jnp.int32, x.shape, 0)
            return jnp.where(x == thr, pos, far)

        need = kf - count(lambda x, p0: x > thr)

        def pos_step(i, last):
            cand = last | lax.shift_left(jnp.int32(1), 11 - i)
            below = count(lambda x, p0: tie_pos(x, p0) < cand)
            return jnp.where(below < need, cand, last)

        last = lax.fori_loop(0, 12, pos_step, jnp.zeros((1, QB), jnp.int32))
        last = jnp.where(has_tie, last, far - 1)
        write_bias(lambda x, p0: jnp.where(x > thr, 1.0,
                                           jnp.where(tie_pos(x, p0) <= last, 1.0, 0.0)) > 0.5)

    m_scr[...] = jnp.full(m_scr.shape, NEG_BIG, jnp.float32)
    l_scr[...] = jnp.zeros(l_scr.shape, jnp.float32)
    acc_scr[...] = jnp.zeros(acc_scr.shape, jnp.float32)

    def attend(bias, k_c, v_c):
        kw = k_c.shape[0]
        for g in range(N_KV_HEADS):
            sl = slice(g * LANES, (g + 1) * LANES)
            s = _dot_nt(qs_scr[g], k_c[:, sl])
            s = jnp.concatenate([s[r * QB:(r + 1) * QB] + bias for r in range(rep)], axis=0)
            m_prev = m_scr[g]
            m_new = jnp.maximum(m_prev, jnp.max(s, axis=-1, keepdims=True))
            alpha = jnp.exp2(m_prev - m_new)
            p = jnp.exp2(s - jnp.concatenate([m_new] * (kw // LANES), axis=1))
            l_scr[g] = alpha * l_scr[g] + jnp.sum(p, axis=-1, keepdims=True)
            acc_scr[g] = alpha * acc_scr[g] + _mm(p.astype(jnp.bfloat16), v_c[:, sl])
            m_scr[g] = m_new

    attend(biasm_scr[...], km_ref[...], vm_ref[...])

    def attend_chunk(c, carry):
        rows = pl.ds(pl.multiple_of(c * KC, KC), KC)
        attend(bias_scr[c], k_ref[0, rows, :], v_ref[0, rows, :])
        return carry

    lax.fori_loop(0, n_seq, attend_chunk, 0)

    for g in range(N_KV_HEADS):
        o = acc_scr[g] / l_scr[g]
        for r in range(rep):
            hd = g * rep + r
            o_ref[0, :, hd * LANES:(hd + 1) * LANES] = o[r * QB:(r + 1) * QB].astype(o_ref.dtype)


def _cast_spec(w, nb, nq):
    rows, cols = w.shape
    if rows % (nb * nq * 16) == 0:
        tr = rows // (nb * nq)
        return pl.BlockSpec((tr, cols), lambda bi, qi_: (bi * nq + qi_, 0))
    assert rows % (nb * 16) == 0 and cols % (nq * LANES) == 0, w.shape
    return pl.BlockSpec((rows // nb, cols // nq), lambda bi, qi_: (bi, qi_))


def _attention(q, qi, wi, k, v, ki, km, vm, kim, k_sel, cast_weights):
    b, s, _ = q.shape
    nq = s // QB
    blk = lambda bi, qi_: (bi, qi_, 0)
    full = lambda bi, qi_: (bi, 0, 0)
    const = lambda bi, qi_: (0, 0)
    rep = N_HEADS // N_KV_HEADS
    f32 = jnp.float32
    cast_specs = [_cast_spec(w, b, nq) for w in cast_weights]
    return pl.pallas_call(
        functools.partial(_attn_kernel, k_sel, len(cast_weights)),
        out_shape=(jax.ShapeDtypeStruct((b, s, ATTN_WIDTH), jnp.bfloat16),
                   *[jax.ShapeDtypeStruct(w.shape, jnp.bfloat16) for w in cast_weights]),
        grid=(b, nq),
        in_specs=[
            pl.BlockSpec((1, QB, ATTN_WIDTH), blk),
            pl.BlockSpec((1, QB, IDX_WIDTH), blk),
            pl.BlockSpec((1, QB, LANES), blk),
            pl.BlockSpec((1, s, KV_WIDTH), full),
            pl.BlockSpec((1, s, KV_WIDTH), full),
            pl.BlockSpec((1, s, LANES), full),
            pl.BlockSpec((MW, KV_WIDTH), const),
            pl.BlockSpec((MW, KV_WIDTH), const),
            pl.BlockSpec((MW, LANES), const),
            *cast_specs,
        ],
        out_specs=(pl.BlockSpec((1, QB, ATTN_WIDTH), blk), *cast_specs),
        scratch_shapes=[
            pltpu.VMEM((N_KV_HEADS, rep * QB, LANES), jnp.bfloat16),
            pltpu.VMEM((N_IDX_HEADS // 4, 4 * QB, LANES), jnp.bfloat16),
            pltpu.VMEM((MW, QB), f32),
            pltpu.VMEM((s // KC, KC, QB), f32),
            pltpu.VMEM((QB, MW), f32),
            pltpu.VMEM((s // KC, QB, KC), f32),
            pltpu.VMEM((N_KV_HEADS, rep * QB, LANES), f32),
            pltpu.VMEM((N_KV_HEADS, rep * QB, LANES), f32),
            pltpu.VMEM((N_KV_HEADS, rep * QB, LANES), f32),
        ],
        compiler_params=_params(("parallel", "arbitrary")),
        name="dsa_attention",
    )(q, qi, wi, k, v, ki, km, vm, kim, *cast_weights)


def _merge_kernel(tiles_per_seq, h_ref, attn_ref, xc_ref, gb_ref, cg_ref, ga0_ref, ga1_ref,
                  gc0_ref, gc1_ref, xc_prev_ref, cg_prev_ref, xc_meta_ref, cg_meta_ref,
                  cw_ref, cb_ref, wa_ref, wc_ref, wo_ref, o_ref):
    f32 = jnp.float32
    first = pl.program_id(0) % tiles_per_seq == 0
    gx = cg_ref[...].astype(f32) * xc_ref[...].astype(f32)
    halo_prev = cg_prev_ref[...].astype(f32) * xc_prev_ref[...].astype(f32)
    halo_meta = cg_meta_ref[...].astype(f32) * xc_meta_ref[...].astype(f32)
    halo = jnp.where(first, halo_meta, halo_prev)
    h1, h2 = halo[HALO - 1:HALO], halo[HALO - 2:HALO - 1]
    row = lax.broadcasted_iota(jnp.int32, gx.shape, 0)
    prev1 = jnp.where(row == 0, h1, pltpu.roll(gx, 1, 0))
    prev2 = jnp.where(row == 0, h2, jnp.where(row == 1, h1, pltpu.roll(gx, 2, 0)))
    cw = cw_ref[...]
    conv = cw[0:1] * prev2 + cw[1:2] * prev1 + cw[2:3] * gx + cb_ref[...]
    feat = (gb_ref[...].astype(f32) * conv).astype(jnp.bfloat16)
    y_conv = _mm(feat, wc_ref[...])
    y_attn = _mm(attn_ref[...], wa_ref[...])
    half = D_MODEL // 2
    sig = lambda r: jax.nn.sigmoid(r[...].astype(f32))
    merged = jnp.concatenate(
        [sig(ga0_ref) * y_attn[:, :half] + sig(gc0_ref) * y_conv[:, :half],
         sig(ga1_ref) * y_attn[:, half:] + sig(gc1_ref) * y_conv[:, half:]], axis=1)
    o_ref[...] = h_ref[...] + _mm(merged.astype(jnp.bfloat16), wo_ref[...])


def _merge(h, attn, zg, zg_meta, conv_w, conv_b, wa, wc, wo, seq):
    n = h.shape[0]
    tm = MERGE_TM
    tiles_per_seq = seq // tm
    row2 = lambda i: (i, 0)
    const = lambda i: (0, 0)
    col = lambda c: pl.BlockSpec((tm, CONV_WIDTH), lambda i: (i, c))
    prev = lambda c: pl.BlockSpec((HALO, CONV_WIDTH),
                                  lambda i: (jnp.maximum(i * (tm // HALO) - 1, 0), c))
    meta = lambda c: pl.BlockSpec((HALO, CONV_WIDTH), lambda i: (N_META // HALO - 1, c))
    single = pl.Buffered(1)
    return pl.pallas_call(
        functools.partial(_merge_kernel, tiles_per_seq),
        out_shape=jax.ShapeDtypeStruct((n, D_MODEL), jnp.float32),
        grid=(n // tm,),
        in_specs=[
            pl.BlockSpec((tm, D_MODEL), row2),
            pl.BlockSpec((tm, ATTN_WIDTH), row2),
            col(0), col(1), col(2), col(3), col(4), col(5), col(6),
            prev(0), prev(2), meta(0), meta(2),
            pl.BlockSpec((CONV_K, CONV_WIDTH), const),
            pl.BlockSpec((1, CONV_WIDTH), const),
            pl.BlockSpec((ATTN_WIDTH, D_MODEL), const, pipeline_mode=single),
            pl.BlockSpec((CONV_WIDTH, D_MODEL), const, pipeline_mode=single),
            pl.BlockSpec((D_MODEL, D_MODEL), const, pipeline_mode=single),
        ],
        out_specs=pl.BlockSpec((tm, D_MODEL), row2),
        compiler_params=_params(("parallel",)),
        name="merge",
    )(h, attn, zg, zg, zg, zg, zg, zg, zg, zg, zg, zg_meta, zg_meta, conv_w, conv_b, wa, wc, wo)


def _rope_tables(pos, head_dim):
    rot = head_dim // ROT_DIV
    half = rot // 2
    inv = ROPE_THETA ** (-jnp.arange(0, rot, 2, dtype=jnp.float32) / rot)
    ang = pos.astype(jnp.float32)[:, None] * inv[None, :]
    cos, sin = jnp.cos(ang), jnp.sin(ang)
    n = pos.shape[0]
    pad = head_dim - rot
    c = jnp.concatenate([cos, cos, jnp.ones((n, pad), jnp.float32)], axis=1)
    zeros_h = jnp.zeros((n, half), jnp.float32)
    zeros_p = jnp.zeros((n, pad), jnp.float32)
    s_lo = jnp.concatenate([-sin, zeros_h, zeros_p], axis=1)
    s_hi = jnp.concatenate([zeros_h, sin, zeros_p], axis=1)
    tile = LANES // head_dim
    return [jnp.tile(t, (1, tile)) for t in (c, s_lo, s_hi)]


def kernel(x, meta_tokens, ffn1_norm_g, ffn1_w_gate, ffn1_w_up, ffn1_w_down, mix_norm_g, w_in, q_norm_g, k_norm_g, conv_w, conv_b, w_attn_branch, w_conv_branch, w_out, ffn2_norm_g, ffn2_w_gate, ffn2_w_up, ffn2_w_down):
    bsz, seq, d = x.shape
    bf = jnp.bfloat16
    k_sel = min(TOPK_MAX, seq // 4)
    assert d == D_MODEL and seq % QB == 0 and k_sel <= KC

    widths = [ATTN_WIDTH, KV_WIDTH, KV_WIDTH, IDX_WIDTH, IDX_DIM, N_IDX_HEADS,
              CONV_WIDTH, CONV_WIDTH, CONV_WIDTH, D_MODEL, D_MODEL]
    offs = [int(o) for o in np.concatenate([[0], np.cumsum(widths)])]
    seg = lambda w, i, j=None: w[:, offs[i]:offs[i + 1 if j is None else j]].astype(bf)

    h = x.reshape(bsz * seq, d)
    hm = meta_tokens.astype(x.dtype)
    pos_seq = jnp.arange(N_META, N_META + seq, dtype=jnp.int32)
    pos_meta = jnp.arange(N_META, dtype=jnp.int32)
    tabs_seq = jnp.stack(_rope_tables(pos_seq, HEAD_DIM) + _rope_tables(pos_seq, IDX_DIM))
    tabs_meta = jnp.stack(_rope_tables(pos_meta, HEAD_DIM) + _rope_tables(pos_meta, IDX_DIM))

    for l in range(ffn1_norm_g.shape[0]):
        wl = w_in[l]
        w_attn = (seg(wl, 0), seg(wl, 3), seg(wl, 1), seg(wl, 2),
                  jnp.pad(seg(wl, 4, 6), ((0, 0), (0, LANES - IDX_DIM - N_IDX_HEADS))))
        w_g = seg(wl, 6, 11)
        g1 = ffn1_norm_g[l][None]
        gm = mix_norm_g[l][None]
        wg1, wu1, wd1 = ffn1_w_gate[l].astype(bf), ffn1_w_up[l].astype(bf), ffn1_w_down[l].astype(bf)

        h = _ffn(h, g1, wg1, wu1, wd1)
        hm = _ffn(hm, g1, wg1, wu1, wd1)

        gq, gk = q_norm_g[l][None], k_norm_g[l][None]
        q, qi, k, v, ki, wi = _proj_attn(h, gm, *w_attn, gq, gk, tabs_seq)
        _, _, km, vm, kim, _ = _proj_attn(hm, gm, *w_attn, gq, gk, tabs_meta)
        zg = _proj_gate(h, gm, w_g)
        zg_meta = _proj_gate(hm, gm, w_g)

        pad_rows = lambda a: jnp.pad(a, ((0, MW - N_META), (0, 0)))
        r3 = lambda a: a.reshape(bsz, seq, a.shape[-1])
        later = [w_attn_branch[l], w_conv_branch[l], w_out[l],
                 ffn2_w_gate[l], ffn2_w_up[l], ffn2_w_down[l]]
        attn, wa, wc, wo, wg2, wu2, wd2 = _attention(
            r3(q), r3(qi), r3(wi), r3(k), r3(v), r3(ki),
            pad_rows(km), pad_rows(vm), pad_rows(kim), k_sel, later)
        attn = attn.reshape(bsz * seq, ATTN_WIDTH)

        h = _merge(h, attn, zg, zg_meta, conv_w[l], conv_b[l][None], wa, wc, wo, seq)

        h = _ffn(h, ffn2_norm_g[l][None], wg2, wu2, wd2)

    return h.reshape(bsz, seq, d)
```

```python
import functools
import math

import jax
import jax.numpy as jnp
import numpy as np
from jax import lax
from jax.experimental import pallas as pl
from jax.experimental.pallas import tpu as pltpu

D_MODEL = 2048
N_META = 16
N_HEADS = 8
N_KV_HEADS = 2
HEAD_DIM = 128
ATTN_WIDTH = N_HEADS * HEAD_DIM
KV_WIDTH = N_KV_HEADS * HEAD_DIM
N_IDX_HEADS = 16
IDX_DIM = 64
IDX_WIDTH = N_IDX_HEADS * IDX_DIM
TOPK_MAX = 256
CONV_WIDTH = D_MODEL // 2
CONV_K = 3
D_FF = 5632
ROPE_THETA = 500000.0
ROT_DIV = 4
EPS = 1e-6

LANES = 128
VMEM_LIMIT = 56 * 1024 * 1024
NEG_BIG = -1e30
Q_SCALE = math.log2(math.e) / math.sqrt(HEAD_DIM)

FFN_TM, FFN_TF = 1024, 512
PROJ_TM = 256
GATE_TM, GATE_TN = 1024, 1024
MERGE_TM = 256
QB = 256
KC = 256
MW = 128
BISECT_STEPS = 3
HALO = 16


def _params(sem):
    return pltpu.CompilerParams(dimension_semantics=sem, vmem_limit_bytes=VMEM_LIMIT)


def _rms(x, g):
    ms = jnp.mean(x * x, axis=-1, keepdims=True)
    return x * lax.rsqrt(ms + EPS) * g


def _mm(a, b):
    return jnp.dot(a, b, preferred_element_type=jnp.float32)


def _ffn_kernel(emit_weights, h_ref, g_ref, wg_ref, wu_ref, wd_ref, o_ref, *rest):
    u_scr = rest[-1]
    j = pl.program_id(1)

    @pl.when(j == 0)
    def _():
        h = h_ref[...]
        u_scr[...] = _rms(h, g_ref[...]).astype(jnp.bfloat16)
        o_ref[...] = h

    bf = jnp.bfloat16
    wg, wu, wd = wg_ref[...].astype(bf), wu_ref[...].astype(bf), wd_ref[...].astype(bf)
    if emit_weights:
        for dst, w in zip(rest[:3], (wg, wu, wd)):
            dst[...] = w
    u = u_scr[...]
    gate = _mm(u, wg)
    up = _mm(u, wu)
    a = (gate * jax.nn.sigmoid(gate)) * (up * 0.5)
    o_ref[...] += _mm(a.astype(bf), wd)


def _ffn(h, g, wg, wu, wd, emit_weights=False):
    n = h.shape[0]
    tm = min(FFN_TM, n)
    assert not emit_weights or n == tm
    w_specs = [
        pl.BlockSpec((D_MODEL, FFN_TF), lambda i, j: (0, j)),
        pl.BlockSpec((D_MODEL, FFN_TF), lambda i, j: (0, j)),
        pl.BlockSpec((FFN_TF, D_MODEL), lambda i, j: (j, 0)),
    ]
    out_shape = [jax.ShapeDtypeStruct((n, D_MODEL), jnp.float32)]
    out_specs = [pl.BlockSpec((tm, D_MODEL), lambda i, j: (i, 0))]
    if emit_weights:
        out_shape += [jax.ShapeDtypeStruct(w.shape, jnp.bfloat16) for w in (wg, wu, wd)]
        out_specs += w_specs
    return pl.pallas_call(
        functools.partial(_ffn_kernel, emit_weights),
        out_shape=out_shape,
        grid=(n // tm, D_FF // FFN_TF),
        in_specs=[
            pl.BlockSpec((tm, D_MODEL), lambda i, j: (i, 0)),
            pl.BlockSpec((1, D_MODEL), lambda i, j: (0, 0)),
            *w_specs,
        ],
        out_specs=out_specs,
        scratch_shapes=[pltpu.VMEM((tm, D_MODEL), jnp.bfloat16)],
        compiler_params=_params(("parallel", "arbitrary")),
        name="ffn",
    )(h, g, wg, wu, wd)


def _rope(x, cos, s_lo, s_hi, half):
    n = x.shape[-1]
    return x * cos + pltpu.roll(x, n - half, 1) * s_lo + pltpu.roll(x, half, 1) * s_hi


def _proj_attn_kernel(h_ref, g_ref, wq_ref, wqi_ref, wk_ref, wv_ref, wkw_ref, gq_ref, gk_ref,
                      tab_ref, q_ref, qi_ref, k_ref, v_ref, ki_ref, wi_ref):
    bf = jnp.bfloat16
    u = _rms(h_ref[...], g_ref[...]).astype(bf)
    ca, sa_lo, sa_hi = tab_ref[0], tab_ref[1], tab_ref[2]
    ci, si_lo, si_hi = tab_ref[3], tab_ref[4], tab_ref[5]
    half_a = HEAD_DIM // ROT_DIV // 2
    half_i = IDX_DIM // ROT_DIV // 2
    gq = gq_ref[...]
    gk = gk_ref[...]
    z = _mm(u, wq_ref[...])
    for hd in range(N_HEADS):
        sl = slice(hd * LANES, (hd + 1) * LANES)
        q = _rope(_rms(z[:, sl], gq), ca, sa_lo, sa_hi, half_a)
        q_ref[:, sl] = (q * Q_SCALE).astype(bf)
    z = _mm(u, wqi_ref[...])
    for p in range(IDX_WIDTH // LANES):
        sl = slice(p * LANES, (p + 1) * LANES)
        qi_ref[:, sl] = _rope(z[:, sl], ci, si_lo, si_hi, half_i).astype(bf)
    z = _mm(u, wk_ref[...])
    for hd in range(N_KV_HEADS):
        sl = slice(hd * LANES, (hd + 1) * LANES)
        k_ref[:, sl] = _rope(_rms(z[:, sl], gk), ca, sa_lo, sa_hi, half_a).astype(bf)
    v_ref[...] = _mm(u, wv_ref[...]).astype(bf)
    x = _mm(u, wkw_ref[...])
    lane = lax.broadcasted_iota(jnp.int32, x.shape, 1)
    is_ki = lane < IDX_DIM
    roped = _rope(x, jnp.where(is_ki, ci, 1.0), jnp.where(is_ki, si_lo, 0.0),
                  jnp.where(is_ki, si_hi, 0.0), half_i)
    swapped = pltpu.roll(roped, IDX_DIM, 1)
    ki_ref[...] = jnp.where(is_ki, roped, swapped).astype(bf)
    wi_ref[...] = swapped


def _proj_attn(h, g, wq, wqi, wk, wv, wkw, gq, gk, tabs):
    n = h.shape[0]
    tm = min(PROJ_TM, n)
    nt = tabs.shape[1] // tm
    row = lambda i: (i, 0)
    const = lambda i: (0, 0)
    bf = jnp.bfloat16
    weight = lambda w: pl.BlockSpec(w.shape, const)
    return pl.pallas_call(
        _proj_attn_kernel,
        out_shape=(
            jax.ShapeDtypeStruct((n, ATTN_WIDTH), bf),
            jax.ShapeDtypeStruct((n, IDX_WIDTH), bf),
            jax.ShapeDtypeStruct((n, KV_WIDTH), bf),
            jax.ShapeDtypeStruct((n, KV_WIDTH), bf),
            jax.ShapeDtypeStruct((n, LANES), bf),
            jax.ShapeDtypeStruct((n, LANES), jnp.float32),
        ),
        grid=(n // tm,),
        in_specs=[
            pl.BlockSpec((tm, D_MODEL), row),
            pl.BlockSpec((1, D_MODEL), const),
            weight(wq), weight(wqi), weight(wk), weight(wv), weight(wkw),
            pl.BlockSpec((1, LANES), const),
            pl.BlockSpec((1, LANES), const),
            pl.BlockSpec((6, tm, LANES), lambda i: (0, i % nt, 0)),
        ],
        out_specs=(
            pl.BlockSpec((tm, ATTN_WIDTH), row),
            pl.BlockSpec((tm, IDX_WIDTH), row),
            pl.BlockSpec((tm, KV_WIDTH), row),
            pl.BlockSpec((tm, KV_WIDTH), row),
            pl.BlockSpec((tm, LANES), row),
            pl.BlockSpec((tm, LANES), row),
        ),
        compiler_params=_params(("parallel",)),
        name="proj_attn",
    )(h, g, wq, wqi, wk, wv, wkw, gq, gk, tabs)


def _proj_gate_kernel(h_ref, g_ref, w_ref, o_ref, u_scr):
    @pl.when(pl.program_id(1) == 0)
    def _():
        u_scr[...] = _rms(h_ref[...], g_ref[...]).astype(jnp.bfloat16)

    o_ref[...] = _mm(u_scr[...], w_ref[...]).astype(o_ref.dtype)


def _proj_gate(h, g, w):
    n = h.shape[0]
    tm = min(GATE_TM, n)
    cols = w.shape[1]
    return pl.pallas_call(
        _proj_gate_kernel,
        out_shape=jax.ShapeDtypeStruct((n, cols), jnp.bfloat16),
        grid=(n // tm, cols // GATE_TN),
        in_specs=[
            pl.BlockSpec((tm, D_MODEL), lambda i, j: (i, 0)),
            pl.BlockSpec((1, D_MODEL), lambda i, j: (0, 0)),
            pl.BlockSpec((D_MODEL, GATE_TN), lambda i, j: (0, j)),
        ],
        out_specs=pl.BlockSpec((tm, GATE_TN), lambda i, j: (i, j)),
        scratch_shapes=[pltpu.VMEM((tm, D_MODEL), jnp.bfloat16)],
        compiler_params=_params(("parallel", "arbitrary")),
        name="proj_gate",
    )(h, g, w)


def _dot_nt(a, b):
    return lax.dot_general(a, b, (((1,), (1,)), ((), ())), preferred_element_type=jnp.float32)


def _attn_kernel(k_sel, n_cast, q_ref, qi_ref, wi_ref, k_ref, v_ref, ki_ref, km_ref, vm_ref,
                 kim_ref, *rest):
    cast_in, o_ref, cast_out = rest[:n_cast], rest[n_cast], rest[n_cast + 1:2 * n_cast + 1]
    (qs_scr, qis_scr, sctm_scr, sct_scr, biasm_scr, bias_scr,
     m_scr, l_scr, acc_scr) = rest[2 * n_cast + 1:]
    for src, dst in zip(cast_in, cast_out):
        dst[...] = src[...].astype(dst.dtype)

    qb = pl.program_id(1)
    n_seq = qb + 1
    n_pairs = (n_seq + 1) // 2
    rep = N_HEADS // N_KV_HEADS
    hpg = 4
    lane_q = lax.broadcasted_iota(jnp.int32, (QB, LANES), 1)
    neg_inf = float("-inf")

    for g in range(N_KV_HEADS):
        for r in range(rep):
            hd = g * rep + r
            qs_scr[g, r * QB:(r + 1) * QB, :] = q_ref[0, :, hd * LANES:(hd + 1) * LANES]
    for i in range(N_IDX_HEADS // hpg):
        for r in range(hpg):
            hd = i * hpg + r
            pair = qi_ref[0, :, (hd // 2) * LANES:(hd // 2 + 1) * LANES]
            keep = (lane_q < IDX_DIM) if hd % 2 == 0 else (lane_q >= IDX_DIM)
            qis_scr[i, r * QB:(r + 1) * QB, :] = jnp.where(keep, pair, jnp.zeros_like(pair))

    w_all = wi_ref[0]

    def index_scores(ki_c):
        kw = ki_c.shape[0]
        score = jnp.zeros((QB, kw), jnp.float32)
        for i in range(N_IDX_HEADS // hpg):
            d = _dot_nt(qis_scr[i], ki_c)
            for r in range(hpg):
                hd = i * hpg + r
                score = score + jnp.maximum(d[r * QB:(r + 1) * QB], 0.0) * w_all[:, hd:hd + 1]
        return score

    lane_m = lax.broadcasted_iota(jnp.int32, (QB, MW), 1)
    sctm_scr[...] = jnp.where(lane_m < N_META, index_scores(kim_ref[...]), neg_inf).T
    row = lax.broadcasted_iota(jnp.int32, (QB, KC), 0)
    lane = lax.broadcasted_iota(jnp.int32, (QB, KC), 1)

    def score_chunk(c):
        ki_c = ki_ref[0, pl.ds(pl.multiple_of(c * KC, KC), KC), :]
        causal = (c - qb) * KC + lane <= row
        sct_scr[c] = jnp.where(causal, index_scores(ki_c), neg_inf).T

    def score_pair(i, carry):
        score_chunk(2 * i)
        score_chunk(2 * i + 1)
        return carry

    lax.fori_loop(0, n_seq // 2, score_pair, 0)

    @pl.when(n_seq % 2 == 1)
    def _():
        score_chunk(n_seq - 1)
        sct_scr[n_seq] = jnp.full((KC, QB), neg_inf, jnp.float32)

    def over_keys(f, init):
        def pair(i, a):
            c = 2 * i
            a = f(sct_scr[c], MW + c * KC, a)
            return f(sct_scr[c + 1], MW + (c + 1) * KC, a)
        return lax.fori_loop(0, n_pairs, pair, f(sctm_scr[...], 0, init))

    def fold8(x):
        return jnp.sum(x.reshape(x.shape[0] // 8, 8, QB), axis=0)

    def count(pred):
        part = over_keys(lambda x, p0, a: a + fold8(jnp.where(pred(x, p0), 1.0, 0.0)),
                         jnp.zeros((8, QB), jnp.float32))
        return jnp.sum(part, axis=0, keepdims=True)

    kf = float(k_sel)
    qlane = lax.broadcasted_iota(jnp.int32, (1, QB), 1)
    n_valid = (N_META + 1 + qb * QB + qlane).astype(jnp.float32)

    def minmax(x, p0, carry):
        mn, mx = carry
        x3 = x.reshape(x.shape[0] // 8, 8, QB)
        mx = jnp.maximum(mx, jnp.max(x3, axis=0))
        mn = jnp.minimum(mn, jnp.min(jnp.where(x3 == neg_inf, float("inf"), x3), axis=0))
        return mn, mx

    mn, mx = over_keys(minmax, (jnp.full((8, QB), float("inf"), jnp.float32),
                                jnp.full((8, QB), neg_inf, jnp.float32)))
    mn = jnp.min(mn, axis=0, keepdims=True)
    mx = jnp.max(mx, axis=0, keepdims=True)

    def count_ge(t):
        return count(lambda x, p0: x >= t)

    def midpoint(lo, hi):
        return lo + (hi - lo) * 0.5

    def is_active(lo, hi, cnt_lo):
        mid = midpoint(lo, hi)
        return jnp.where((cnt_lo != kf) & (mid > lo) & (mid < hi), 1.0, 0.0)

    cnt_mx = count_ge(mx)
    at_max = cnt_mx >= kf
    lo0 = jnp.where(at_max, mx, mn)
    cnt0 = jnp.where(at_max, cnt_mx, n_valid)
    act0 = jnp.where(n_valid > kf, is_active(lo0, mx, cnt0), 0.0)

    def bisect_once(state):
        lo, hi, cnt_lo, act = state
        mid = midpoint(lo, hi)
        cnt = count_ge(mid)
        up = (act > 0.5) & (cnt >= kf)
        down = (act > 0.5) & (cnt < kf)
        lo = jnp.where(up, mid, lo)
        cnt_lo = jnp.where(up, cnt, cnt_lo)
        hi = jnp.where(down, mid, hi)
        return lo, hi, cnt_lo, act * is_active(lo, hi, cnt_lo)

    def bisect_trip(st):
        for _ in range(BISECT_STEPS):
            st = bisect_once(st)
        return st

    thr, _, cnt_thr, _ = lax.while_loop(lambda st: jnp.max(st[3]) > 0.5, bisect_trip,
                                        (lo0, mx, cnt0, act0))
    has_tie = cnt_thr > kf
    any_tie = jnp.max(jnp.where(has_tie, 1.0, 0.0)) > 0.5

    def write_bias(keep):
        biasm_scr[...] = jnp.where(keep(sctm_scr[...], 0), 0.0, NEG_BIG).T

        def body(c, carry):
            bias_scr[c] = jnp.where(keep(sct_scr[c], MW + c * KC), 0.0, NEG_BIG).T
            return carry
        lax.fori_loop(0, n_seq, body, 0)

    @pl.when(jnp.logical_not(any_tie))
    def _():
        write_bias(lambda x, p0: x >= thr)

    @pl.when(any_tie)
    def _():
        far = 2 ** 30

        def tie_pos(x, p0):
            pos = p0 + lax.broadcasted_iota(jnp.int32, x.shape, 0)
            return jnp.where(x == thr, pos, far)

        need = kf - count(lambda x, p0: x > thr)

        def pos_step(i, last):
            cand = last | lax.shift_left(jnp.int32(1), 11 - i)
            below = count(lambda x, p0: tie_pos(x, p0) < cand)
            return jnp.where(below < need, cand, last)

        last = lax.fori_loop(0, 12, pos_step, jnp.zeros((1, QB), jnp.int32))
        last = jnp.where(has_tie, last, far - 1)
        write_bias(lambda x, p0: jnp.where(x > thr, 1.0,
                                           jnp.where(tie_pos(x, p0) <= last, 1.0, 0.0)) > 0.5)

    m_scr[...] = jnp.full(m_scr.shape, NEG_BIG, jnp.float32)
    l_scr[...] = jnp.zeros(l_scr.shape, jnp.float32)
    acc_scr[...] = jnp.zeros(acc_scr.shape, jnp.float32)

    def attend(bias, k_c, v_c):
        kw = k_c.shape[0]
        for g in range(N_KV_HEADS):
            sl = slice(g * LANES, (g + 1) * LANES)
            s = _dot_nt(qs_scr[g], k_c[:, sl])
            s = jnp.concatenate([s[r * QB:(r + 1) * QB] + bias for r in range(rep)], axis=0)
            m_prev = m_scr[g]
            m_new = jnp.maximum(m_prev, jnp.max(s, axis=-1, keepdims=True))
            alpha = jnp.exp2(m_prev - m_new)
            p = jnp.exp2(s - jnp.concatenate([m_new] * (kw // LANES), axis=1))
            l_scr[g] = alpha * l_scr[g] + jnp.sum(p, axis=-1, keepdims=True)
            acc_scr[g] = alpha * acc_scr[g] + _mm(p.astype(jnp.bfloat16), v_c[:, sl])
            m_scr[g] = m_new

    attend(biasm_scr[...], km_ref[...], vm_ref[...])

    def attend_pair(i, carry):
        c = 2 * i
        rows = pl.ds(pl.multiple_of(c * KC, 2 * KC), 2 * KC)
        attend(jnp.concatenate([bias_scr[c], bias_scr[c + 1]], axis=1),
               k_ref[0, rows, :], v_ref[0, rows, :])
        return carry

    lax.fori_loop(0, n_seq // 2, attend_pair, 0)

    @pl.when(n_seq % 2 == 1)
    def _():
        c = n_seq - 1
        rows = pl.ds(pl.multiple_of(c * KC, KC), KC)
        attend(bias_scr[c], k_ref[0, rows, :], v_ref[0, rows, :])

    for g in range(N_KV_HEADS):
        o = acc_scr[g] / l_scr[g]
        for r in range(rep):
            hd = g * rep + r
            o_ref[0, :, hd * LANES:(hd + 1) * LANES] = o[r * QB:(r + 1) * QB].astype(o_ref.dtype)


def _cast_spec(w, nb, nq):
    rows, cols = w.shape
    if rows % (nb * nq * 16) == 0:
        tr = rows // (nb * nq)
        return pl.BlockSpec((tr, cols), lambda bi, qi_: (bi * nq + qi_, 0))
    assert rows % (nb * 16) == 0 and cols % (nq * LANES) == 0, w.shape
    return pl.BlockSpec((rows // nb, cols // nq), lambda bi, qi_: (bi, qi_))


def _attention(q, qi, wi, k, v, ki, km, vm, kim, k_sel, cast_weights):
    b, s, _ = q.shape
    nq = s // QB
    blk = lambda bi, qi_: (bi, qi_, 0)
    full = lambda bi, qi_: (bi, 0, 0)
    const = lambda bi, qi_: (0, 0)
    rep = N_HEADS // N_KV_HEADS
    f32 = jnp.float32
    cast_specs = [_cast_spec(w, b, nq) for w in cast_weights]
    return pl.pallas_call(
        functools.partial(_attn_kernel, k_sel, len(cast_weights)),
        out_shape=(jax.ShapeDtypeStruct((b, s, ATTN_WIDTH), jnp.bfloat16),
                   *[jax.ShapeDtypeStruct(w.shape, jnp.bfloat16) for w in cast_weights]),
        grid=(b, nq),
        in_specs=[
            pl.BlockSpec((1, QB, ATTN_WIDTH), blk),
            pl.BlockSpec((1, QB, IDX_WIDTH), blk),
            pl.BlockSpec((1, QB, LANES), blk),
            pl.BlockSpec((1, s, KV_WIDTH), full),
            pl.BlockSpec((1, s, KV_WIDTH), full),
            pl.BlockSpec((1, s, LANES), full),
            pl.BlockSpec((MW, KV_WIDTH), const),
            pl.BlockSpec((MW, KV_WIDTH), const),
            pl.BlockSpec((MW, LANES), const),
            *cast_specs,
        ],
        out_specs=(pl.BlockSpec((1, QB, ATTN_WIDTH), blk), *cast_specs),
        scratch_shapes=[
            pltpu.VMEM((N_KV_HEADS, rep * QB, LANES), jnp.bfloat16),
            pltpu.VMEM((N_IDX_HEADS // 4, 4 * QB, LANES), jnp.bfloat16),
            pltpu.VMEM((MW, QB), f32),
            pltpu.VMEM((s // KC, KC, QB), f32),
            pltpu.VMEM((QB, MW), f32),
            pltpu.VMEM((s // KC, QB, KC), f32),
            pltpu.VMEM((N_KV_HEADS, rep * QB, LANES), f32),
            pltpu.VMEM((N_KV_HEADS, rep * QB, LANES), f32),
            pltpu.VMEM((N_KV_HEADS, rep * QB, LANES), f32),
        ],
        compiler_params=_params(("parallel", "arbitrary")),
        name="dsa_attention",
    )(q, qi, wi, k, v, ki, km, vm, kim, *cast_weights)


def _merge_kernel(tiles_per_seq, h_ref, attn_ref, xc_ref, gb_ref, cg_ref, ga0_ref, ga1_ref,
                  gc0_ref, gc1_ref, xc_prev_ref, cg_prev_ref, xc_meta_ref, cg_meta_ref,
                  cw_ref, cb_ref, wa_ref, wc_ref, wo_ref, o_ref):
    f32 = jnp.float32
    first = pl.program_id(0) % tiles_per_seq == 0
    gx = cg_ref[...].astype(f32) * xc_ref[...].astype(f32)
    halo_prev = cg_prev_ref[...].astype(f32) * xc_prev_ref[...].astype(f32)
    halo_meta = cg_meta_ref[...].astype(f32) * xc_meta_ref[...].astype(f32)
    halo = jnp.where(first, halo_meta, halo_prev)
    h1, h2 = halo[HALO - 1:HALO], halo[HALO - 2:HALO - 1]
    row = lax.broadcasted_iota(jnp.int32, gx.shape, 0)
    prev1 = jnp.where(row == 0, h1, pltpu.roll(gx, 1, 0))
    prev2 = jnp.where(row == 0, h2, jnp.where(row == 1, h1, pltpu.roll(gx, 2, 0)))
    cw = cw_ref[...]
    conv = cw[0:1] * prev2 + cw[1:2] * prev1 + cw[2:3] * gx + cb_ref[...]
    feat = (gb_ref[...].astype(f32) * conv).astype(jnp.bfloat16)
    y_conv = _mm(feat, wc_ref[...])
    y_attn = _mm(attn_ref[...], wa_ref[...])
    half = D_MODEL // 2
    sig = lambda r: jax.nn.sigmoid(r[...].astype(f32))
    merged = jnp.concatenate(
        [sig(ga0_ref) * y_attn[:, :half] + sig(gc0_ref) * y_conv[:, :half],
         sig(ga1_ref) * y_attn[:, half:] + sig(gc1_ref) * y_conv[:, half:]], axis=1)
    o_ref[...] = h_ref[...] + _mm(merged.astype(jnp.bfloat16), wo_ref[...])


def _merge(h, attn, zg, zg_meta, conv_w, conv_b, wa, wc, wo, seq):
    n = h.shape[0]
    tm = MERGE_TM
    tiles_per_seq = seq // tm
    row2 = lambda i: (i, 0)
    const = lambda i: (0, 0)
    col = lambda c: pl.BlockSpec((tm, CONV_WIDTH), lambda i: (i, c))
    prev = lambda c: pl.BlockSpec((HALO, CONV_WIDTH),
                                  lambda i: (jnp.maximum(i * (tm // HALO) - 1, 0), c))
    meta = lambda c: pl.BlockSpec((HALO, CONV_WIDTH), lambda i: (N_META // HALO - 1, c))
    single = pl.Buffered(1)
    return pl.pallas_call(
        functools.partial(_merge_kernel, tiles_per_seq),
        out_shape=jax.ShapeDtypeStruct((n, D_MODEL), jnp.float32),
        grid=(n // tm,),
        in_specs=[
            pl.BlockSpec((tm, D_MODEL), row2),
            pl.BlockSpec((tm, ATTN_WIDTH), row2),
            col(0), col(1), col(2), col(3), col(4), col(5), col(6),
            prev(0), prev(2), meta(0), meta(2),
            pl.BlockSpec((CONV_K, CONV_WIDTH), const),
            pl.BlockSpec((1, CONV_WIDTH), const),
            pl.BlockSpec((ATTN_WIDTH, D_MODEL), const, pipeline_mode=single),
            pl.BlockSpec((CONV_WIDTH, D_MODEL), const, pipeline_mode=single),
            pl.BlockSpec((D_MODEL, D_MODEL), const, pipeline_mode=single),
        ],
        out_specs=pl.BlockSpec((tm, D_MODEL), row2),
        compiler_params=_params(("parallel",)),
        name="merge",
    )(h, attn, zg, zg, zg, zg, zg, zg, zg, zg, zg, zg_meta, zg_meta, conv_w, conv_b, wa, wc, wo)


def _rope_tables(pos, head_dim):
    rot = head_dim // ROT_DIV
    half = rot // 2
    inv = ROPE_THETA ** (-jnp.arange(0, rot, 2, dtype=jnp.float32) / rot)
    ang = pos.astype(jnp.float32)[:, None] * inv[None, :]
    cos, sin = jnp.cos(ang), jnp.sin(ang)
    n = pos.shape[0]
    pad = head_dim - rot
    c = jnp.concatenate([cos, cos, jnp.ones((n, pad), jnp.float32)], axis=1)
    zeros_h = jnp.zeros((n, half), jnp.float32)
    zeros_p = jnp.zeros((n, pad), jnp.float32)
    s_lo = jnp.concatenate([-sin, zeros_h, zeros_p], axis=1)
    s_hi = jnp.concatenate([zeros_h, sin, zeros_p], axis=1)
    tile = LANES // head_dim
    return [jnp.tile(t, (1, tile)) for t in (c, s_lo, s_hi)]


def kernel(x, meta_tokens, ffn1_norm_g, ffn1_w_gate, ffn1_w_up, ffn1_w_down, mix_norm_g, w_in, q_norm_g, k_norm_g, conv_w, conv_b, w_attn_branch, w_conv_branch, w_out, ffn2_norm_g, ffn2_w_gate, ffn2_w_up, ffn2_w_down):
    bsz, seq, d = x.shape
    bf = jnp.bfloat16
    k_sel = min(TOPK_MAX, seq // 4)
    assert d == D_MODEL and seq % QB == 0 and k_sel <= KC

    widths = [ATTN_WIDTH, KV_WIDTH, KV_WIDTH, IDX_WIDTH, IDX_DIM, N_IDX_HEADS,
              CONV_WIDTH, CONV_WIDTH, CONV_WIDTH, D_MODEL, D_MODEL]
    offs = [int(o) for o in np.concatenate([[0], np.cumsum(widths)])]
    seg = lambda w, i, j=None: w[:, offs[i]:offs[i + 1 if j is None else j]].astype(bf)

    h = x.reshape(bsz * seq, d)
    hm = meta_tokens.astype(x.dtype)
    pos_seq = jnp.arange(N_META, N_META + seq, dtype=jnp.int32)
    pos_meta = jnp.arange(N_META, dtype=jnp.int32)
    tabs_seq = jnp.stack(_rope_tables(pos_seq, HEAD_DIM) + _rope_tables(pos_seq, IDX_DIM))
    tabs_meta = jnp.stack(_rope_tables(pos_meta, HEAD_DIM) + _rope_tables(pos_meta, IDX_DIM))

    for l in range(ffn1_norm_g.shape[0]):
        wl = w_in[l]
        w_attn = (seg(wl, 0), seg(wl, 3), seg(wl, 1), seg(wl, 2),
                  jnp.pad(seg(wl, 4, 6), ((0, 0), (0, LANES - IDX_DIM - N_IDX_HEADS))))
        w_g = seg(wl, 6, 11)
        g1 = ffn1_norm_g[l][None]
        gm = mix_norm_g[l][None]

        hm, wg1, wu1, wd1 = _ffn(hm, g1, ffn1_w_gate[l], ffn1_w_up[l], ffn1_w_down[l],
                                 emit_weights=True)
        h, = _ffn(h, g1, wg1, wu1, wd1)

        gq, gk = q_norm_g[l][None], k_norm_g[l][None]
        q, qi, k, v, ki, wi = _proj_attn(h, gm, *w_attn, gq, gk, tabs_seq)
        _, _, km, vm, kim, _ = _proj_attn(hm, gm, *w_attn, gq, gk, tabs_meta)
        zg = _proj_gate(h, gm, w_g)
        zg_meta = _proj_gate(hm, gm, w_g)

        pad_rows = lambda a: jnp.pad(a, ((0, MW - N_META), (0, 0)))
        r3 = lambda a: a.reshape(bsz, seq, a.shape[-1])
        later = [w_attn_branch[l], w_conv_branch[l], w_out[l],
                 ffn2_w_gate[l], ffn2_w_up[l], ffn2_w_down[l]]
        attn, wa, wc, wo, wg2, wu2, wd2 = _attention(
            r3(q), r3(qi), r3(wi), r3(k), r3(v), r3(ki),
            pad_rows(km), pad_rows(vm), pad_rows(kim), k_sel, later)
        attn = attn.reshape(bsz * seq, ATTN_WIDTH)

        h = _merge(h, attn, zg, zg_meta, conv_w[l], conv_b[l][None], wa, wc, wo, seq)

        h, = _ffn(h, ffn2_norm_g[l][None], wg2, wu2, wd2)

    return h.reshape(bsz, seq, d)
```

```python
import functools
import math

import jax
import jax.numpy as jnp
import numpy as np
from jax import lax
from jax.experimental import pallas as pl
from jax.experimental.pallas import tpu as pltpu

D_MODEL = 2048
N_META = 16
N_HEADS = 8
N_KV_HEADS = 2
HEAD_DIM = 128
ATTN_WIDTH = N_HEADS * HEAD_DIM
KV_WIDTH = N_KV_HEADS * HEAD_DIM
N_IDX_HEADS = 16
IDX_DIM = 64
IDX_WIDTH = N_IDX_HEADS * IDX_DIM
TOPK_MAX = 256
CONV_WIDTH = D_MODEL // 2
CONV_K = 3
D_FF = 5632
ROPE_THETA = 500000.0
ROT_DIV = 4
EPS = 1e-6

LANES = 128
VMEM_LIMIT = 56 * 1024 * 1024
NEG_BIG = -1e30
Q_SCALE = math.log2(math.e) / math.sqrt(HEAD_DIM)

FFN_TM, FFN_TF = 1024, 512
PROJ_TM = 256
GATE_TM, GATE_TN = 1024, 1792
MERGE_TM = 256
QB = 256
KC = 256
MW = 128
BISECT_STEPS = 3
HALO = 16


def _params(sem):
    return pltpu.CompilerParams(dimension_semantics=sem, vmem_limit_bytes=VMEM_LIMIT)


def _rms(x, g):
    ms = jnp.mean(x * x, axis=-1, keepdims=True)
    return x * lax.rsqrt(ms + EPS) * g


def _mm(a, b):
    return jnp.dot(a, b, preferred_element_type=jnp.float32)


def _ffn_kernel(emit_weights, h_ref, g_ref, wg_ref, wu_ref, wd_ref, o_ref, *rest):
    u_scr = rest[-1]
    j = pl.program_id(1)

    @pl.when(j == 0)
    def _():
        h = h_ref[...]
        u_scr[...] = _rms(h, g_ref[...]).astype(jnp.bfloat16)
        o_ref[...] = h

    bf = jnp.bfloat16
    wg, wu, wd = wg_ref[...].astype(bf), wu_ref[...].astype(bf), wd_ref[...].astype(bf)
    if emit_weights:
        for dst, w in zip(rest[:3], (wg, wu, wd)):
            dst[...] = w
    u = u_scr[...]
    gate = _mm(u, wg)
    up = _mm(u, wu)
    a = (gate * jax.nn.sigmoid(gate)) * (up * 0.5)
    o_ref[...] += _mm(a.astype(bf), wd)


def _ffn(h, g, wg, wu, wd, emit_weights=False):
    n = h.shape[0]
    tm = min(FFN_TM, n)
    assert not emit_weights or n == tm
    w_specs = [
        pl.BlockSpec((D_MODEL, FFN_TF), lambda i, j: (0, j)),
        pl.BlockSpec((D_MODEL, FFN_TF), lambda i, j: (0, j)),
        pl.BlockSpec((FFN_TF, D_MODEL), lambda i, j: (j, 0)),
    ]
    out_shape = [jax.ShapeDtypeStruct((n, D_MODEL), jnp.float32)]
    out_specs = [pl.BlockSpec((tm, D_MODEL), lambda i, j: (i, 0))]
    if emit_weights:
        out_shape += [jax.ShapeDtypeStruct(w.shape, jnp.bfloat16) for w in (wg, wu, wd)]
        out_specs += w_specs
    return pl.pallas_call(
        functools.partial(_ffn_kernel, emit_weights),
        out_shape=out_shape,
        grid=(n // tm, D_FF // FFN_TF),
        in_specs=[
            pl.BlockSpec((tm, D_MODEL), lambda i, j: (i, 0)),
            pl.BlockSpec((1, D_MODEL), lambda i, j: (0, 0)),
            *w_specs,
        ],
        out_specs=out_specs,
        scratch_shapes=[pltpu.VMEM((tm, D_MODEL), jnp.bfloat16)],
        compiler_params=_params(("parallel", "arbitrary")),
        name="ffn",
    )(h, g, wg, wu, wd)


def _rope(x, cos, s_lo, s_hi, half):
    n = x.shape[-1]
    return x * cos + pltpu.roll(x, n - half, 1) * s_lo + pltpu.roll(x, half, 1) * s_hi


def _proj_attn_kernel(h_ref, g_ref, wq_ref, wqi_ref, wk_ref, wv_ref, wkw_ref, gq_ref, gk_ref,
                      tab_ref, q_ref, qi_ref, k_ref, v_ref, ki_ref, wi_ref):
    bf = jnp.bfloat16
    u = _rms(h_ref[...], g_ref[...]).astype(bf)
    ca, sa_lo, sa_hi = tab_ref[0], tab_ref[1], tab_ref[2]
    ci, si_lo, si_hi = tab_ref[3], tab_ref[4], tab_ref[5]
    half_a = HEAD_DIM // ROT_DIV // 2
    half_i = IDX_DIM // ROT_DIV // 2
    gq = gq_ref[...]
    gk = gk_ref[...]
    z = _mm(u, wq_ref[...])
    for hd in range(N_HEADS):
        sl = slice(hd * LANES, (hd + 1) * LANES)
        q = _rope(_rms(z[:, sl], gq), ca, sa_lo, sa_hi, half_a)
        q_ref[:, sl] = (q * Q_SCALE).astype(bf)
    z = _mm(u, wqi_ref[...])
    for p in range(IDX_WIDTH // LANES):
        sl = slice(p * LANES, (p + 1) * LANES)
        qi_ref[:, sl] = _rope(z[:, sl], ci, si_lo, si_hi, half_i).astype(bf)
    z = _mm(u, wk_ref[...])
    for hd in range(N_KV_HEADS):
        sl = slice(hd * LANES, (hd + 1) * LANES)
        k_ref[:, sl] = _rope(_rms(z[:, sl], gk), ca, sa_lo, sa_hi, half_a).astype(bf)
    v_ref[...] = _mm(u, wv_ref[...]).astype(bf)
    x = _mm(u, wkw_ref[...])
    lane = lax.broadcasted_iota(jnp.int32, x.shape, 1)
    is_ki = lane < IDX_DIM
    roped = _rope(x, jnp.where(is_ki, ci, 1.0), jnp.where(is_ki, si_lo, 0.0),
                  jnp.where(is_ki, si_hi, 0.0), half_i)
    swapped = pltpu.roll(roped, IDX_DIM, 1)
    ki_ref[...] = jnp.where(is_ki, roped, swapped).astype(bf)
    wi_ref[...] = swapped


def _proj_attn(h, g, wq, wqi, wk, wv, wkw, gq, gk, tabs):
    n = h.shape[0]
    tm = min(PROJ_TM, n)
    nt = tabs.shape[1] // tm
    row = lambda i: (i, 0)
    const = lambda i: (0, 0)
    bf = jnp.bfloat16
    weight = lambda w: pl.BlockSpec(w.shape, const)
    return pl.pallas_call(
        _proj_attn_kernel,
        out_shape=(
            jax.ShapeDtypeStruct((n, ATTN_WIDTH), bf),
            jax.ShapeDtypeStruct((n, IDX_WIDTH), bf),
            jax.ShapeDtypeStruct((n, KV_WIDTH), bf),
            jax.ShapeDtypeStruct((n, KV_WIDTH), bf),
            jax.ShapeDtypeStruct((n, LANES), bf),
            jax.ShapeDtypeStruct((n, LANES), jnp.float32),
        ),
        grid=(n // tm,),
        in_specs=[
            pl.BlockSpec((tm, D_MODEL), row),
            pl.BlockSpec((1, D_MODEL), const),
            weight(wq), weight(wqi), weight(wk), weight(wv), weight(wkw),
            pl.BlockSpec((1, LANES), const),
            pl.BlockSpec((1, LANES), const),
            pl.BlockSpec((6, tm, LANES), lambda i: (0, i % nt, 0)),
        ],
        out_specs=(
            pl.BlockSpec((tm, ATTN_WIDTH), row),
            pl.BlockSpec((tm, IDX_WIDTH), row),
            pl.BlockSpec((tm, KV_WIDTH), row),
            pl.BlockSpec((tm, KV_WIDTH), row),
            pl.BlockSpec((tm, LANES), row),
            pl.BlockSpec((tm, LANES), row),
        ),
        compiler_params=_params(("parallel",)),
        name="proj_attn",
    )(h, g, wq, wqi, wk, wv, wkw, gq, gk, tabs)


def _proj_gate_kernel(h_ref, g_ref, w_ref, o_ref, u_scr):
    @pl.when(pl.program_id(1) == 0)
    def _():
        u_scr[...] = _rms(h_ref[...], g_ref[...]).astype(jnp.bfloat16)

    o_ref[...] = _mm(u_scr[...], w_ref[...]).astype(o_ref.dtype)


def _proj_gate(h, g, w):
    n = h.shape[0]
    tm = min(GATE_TM, n)
    cols = w.shape[1]
    return pl.pallas_call(
        _proj_gate_kernel,
        out_shape=jax.ShapeDtypeStruct((n, cols), jnp.bfloat16),
        grid=(n // tm, cols // GATE_TN),
        in_specs=[
            pl.BlockSpec((tm, D_MODEL), lambda i, j: (i, 0)),
            pl.BlockSpec((1, D_MODEL), lambda i, j: (0, 0)),
            pl.BlockSpec((D_MODEL, GATE_TN), lambda i, j: (0, j)),
        ],
        out_specs=pl.BlockSpec((tm, GATE_TN), lambda i, j: (i, j)),
        scratch_shapes=[pltpu.VMEM((tm, D_MODEL), jnp.bfloat16)],
        compiler_params=_params(("parallel", "arbitrary")),
        name="proj_gate",
    )(h, g, w)


def _dot_nt(a, b):
    return lax.dot_general(a, b, (((1,), (1,)), ((), ())), preferred_element_type=jnp.float32)


def _attn_kernel(k_sel, n_cast, q_ref, qi_ref, wi_ref, k_ref, v_ref, ki_ref, km_ref, vm_ref,
                 kim_ref, *rest):
    cast_in, o_ref, cast_out = rest[:n_cast], rest[n_cast], rest[n_cast + 1:2 * n_cast + 1]
    (qs_scr, qis_scr, sctm_scr, sct_scr, biasm_scr, bias_scr,
     m_scr, l_scr, acc_scr) = rest[2 * n_cast + 1:]
    for src, dst in zip(cast_in, cast_out):
        dst[...] = src[...].astype(dst.dtype)

    qb = pl.program_id(1)
    n_seq = qb + 1
    rep = N_HEADS // N_KV_HEADS
    hpg = 4
    lane_q = lax.broadcasted_iota(jnp.int32, (QB, LANES), 1)
    neg_inf = float("-inf")

    for g in range(N_KV_HEADS):
        for r in range(rep):
            hd = g * rep + r
            qs_scr[g, r * QB:(r + 1) * QB, :] = q_ref[0, :, hd * LANES:(hd + 1) * LANES]
    for i in range(N_IDX_HEADS // hpg):
        for r in range(hpg):
            hd = i * hpg + r
            pair = qi_ref[0, :, (hd // 2) * LANES:(hd // 2 + 1) * LANES]
            keep = (lane_q < IDX_DIM) if hd % 2 == 0 else (lane_q >= IDX_DIM)
            qis_scr[i, r * QB:(r + 1) * QB, :] = jnp.where(keep, pair, jnp.zeros_like(pair))

    w_all = wi_ref[0]

    def index_scores(ki_c):
        kw = ki_c.shape[0]
        score = jnp.zeros((QB, kw), jnp.float32)
        for i in range(N_IDX_HEADS // hpg):
            d = _dot_nt(qis_scr[i], ki_c)
            for r in range(hpg):
                hd = i * hpg + r
                score = score + jnp.maximum(d[r * QB:(r + 1) * QB], 0.0) * w_all[:, hd:hd + 1]
        return score

    lane_m = lax.broadcasted_iota(jnp.int32, (QB, MW), 1)
    sctm_scr[...] = jnp.where(lane_m < N_META, index_scores(kim_ref[...]), neg_inf).T
    row = lax.broadcasted_iota(jnp.int32, (QB, KC), 0)
    lane = lax.broadcasted_iota(jnp.int32, (QB, KC), 1)

    def score_chunk(c):
        ki_c = ki_ref[0, pl.ds(pl.multiple_of(c * KC, KC), KC), :]
        causal = (c - qb) * KC + lane <= row
        s = jnp.where(causal, index_scores(ki_c), neg_inf)
        bias_scr[c] = s
        sct_scr[c] = s.T

    def score_pair(i, carry):
        score_chunk(2 * i)
        score_chunk(2 * i + 1)
        return carry

    lax.fori_loop(0, n_seq // 2, score_pair, 0)

    @pl.when(n_seq % 2 == 1)
    def _():
        score_chunk(n_seq - 1)
        sct_scr[n_seq] = jnp.full((KC, QB), neg_inf, jnp.float32)

    def over_keys(f, init):
        def pair(i, a):
            c = 2 * i
            a = f(sct_scr[c], MW + c * KC, a)
            return f(sct_scr[c + 1], MW + (c + 1) * KC, a)
        return lax.fori_loop(0, (n_seq + 1) // 2, pair, f(sctm_scr[...], 0, init))

    def fold8(x):
        return jnp.sum(x.reshape(x.shape[0] // 8, 8, QB), axis=0)

    def count(pred):
        part = over_keys(lambda x, p0, a: a + fold8(jnp.where(pred(x, p0), 1.0, 0.0)),
                         jnp.zeros((8, QB), jnp.float32))
        return jnp.sum(part, axis=0, keepdims=True)

    kf = float(k_sel)
    qlane = lax.broadcasted_iota(jnp.int32, (1, QB), 1)
    n_valid = (N_META + 1 + qb * QB + qlane).astype(jnp.float32)

    def minmax(x, p0, carry):
        mn, mx = carry
        x3 = x.reshape(x.shape[0] // 8, 8, QB)
        mx = jnp.maximum(mx, jnp.max(x3, axis=0))
        mn = jnp.minimum(mn, jnp.min(jnp.where(x3 == neg_inf, float("inf"), x3), axis=0))
        return mn, mx

    mn, mx = over_keys(minmax, (jnp.full((8, QB), float("inf"), jnp.float32),
                                jnp.full((8, QB), neg_inf, jnp.float32)))
    mn = jnp.min(mn, axis=0, keepdims=True)
    mx = jnp.max(mx, axis=0, keepdims=True)

    def count_ge(t):
        return count(lambda x, p0: x >= t)

    def midpoint(lo, hi):
        return lo + (hi - lo) * 0.5

    def is_active(lo, hi, cnt_lo):
        mid = midpoint(lo, hi)
        return jnp.where((cnt_lo != kf) & (mid > lo) & (mid < hi), 1.0, 0.0)

    cnt_mx = count_ge(mx)
    at_max = cnt_mx >= kf
    lo0 = jnp.where(at_max, mx, mn)
    cnt0 = jnp.where(at_max, cnt_mx, n_valid)
    act0 = jnp.where(n_valid > kf, is_active(lo0, mx, cnt0), 0.0)

    def bisect_once(state):
        lo, hi, cnt_lo, act = state
        mid = midpoint(lo, hi)
        cnt = count_ge(mid)
        up = (act > 0.5) & (cnt >= kf)
        down = (act > 0.5) & (cnt < kf)
        lo = jnp.where(up, mid, lo)
        cnt_lo = jnp.where(up, cnt, cnt_lo)
        hi = jnp.where(down, mid, hi)
        return lo, hi, cnt_lo, act * is_active(lo, hi, cnt_lo)

    def bisect_trip(st):
        for _ in range(BISECT_STEPS):
            st = bisect_once(st)
        return st

    thr, _, cnt_thr, _ = lax.while_loop(lambda st: jnp.max(st[3]) > 0.5, bisect_trip,
                                        (lo0, mx, cnt0, act0))
    has_tie = cnt_thr > kf
    any_tie = jnp.max(jnp.where(has_tie, 1.0, 0.0)) > 0.5

    def write_bias(keep):
        biasm_scr[...] = jnp.where(keep(sctm_scr[...], 0), 0.0, NEG_BIG).T

        def body(c, carry):
            bias_scr[c] = jnp.where(keep(sct_scr[c], MW + c * KC), 0.0, NEG_BIG).T
            return carry
        lax.fori_loop(0, n_seq, body, 0)

    @pl.when(jnp.logical_not(any_tie))
    def _():
        biasm_scr[...] = jnp.where(sctm_scr[...] >= thr, 0.0, NEG_BIG).T
        thr_q = jnp.broadcast_to(thr, (LANES, QB)).T
        thr_q = jnp.concatenate([thr_q] * (KC // LANES), axis=1)

        def body(c, carry):
            bias_scr[c] = jnp.where(bias_scr[c] >= thr_q, 0.0, NEG_BIG)
            return carry
        lax.fori_loop(0, n_seq, body, 0)

    @pl.when(any_tie)
    def _():
        far = 2 ** 30

        def tie_pos(x, p0):
            pos = p0 + lax.broadcasted_iota(jnp.int32, x.shape, 0)
            return jnp.where(x == thr, pos, far)

        need = kf - count(lambda x, p0: x > thr)

        def pos_step(i, last):
            cand = last | lax.shift_left(jnp.int32(1), 11 - i)
            below = count(lambda x, p0: tie_pos(x, p0) < cand)
            return jnp.where(below < need, cand, last)

        last = lax.fori_loop(0, 12, pos_step, jnp.zeros((1, QB), jnp.int32))
        last = jnp.where(has_tie, last, far - 1)
        write_bias(lambda x, p0: jnp.where(x > thr, 1.0,
                                           jnp.where(tie_pos(x, p0) <= last, 1.0, 0.0)) > 0.5)

    m_scr[...] = jnp.full(m_scr.shape, NEG_BIG, jnp.float32)
    l_scr[...] = jnp.zeros(l_scr.shape, jnp.float32)
    acc_scr[...] = jnp.zeros(acc_scr.shape, jnp.float32)

    def attend(bias, k_c, v_c):
        kw = k_c.shape[0]
        for g in range(N_KV_HEADS):
            sl = slice(g * LANES, (g + 1) * LANES)
            s = _dot_nt(qs_scr[g], k_c[:, sl])
            s = jnp.concatenate([s[r * QB:(r + 1) * QB] + bias for r in range(rep)], axis=0)
            m_prev = m_scr[g]
            m_new = jnp.maximum(m_prev, jnp.max(s, axis=-1, keepdims=True))
            alpha = jnp.exp2(m_prev - m_new)
            p = jnp.exp2(s - jnp.concatenate([m_new] * (kw // LANES), axis=1))
            l_scr[g] = alpha * l_scr[g] + jnp.sum(p, axis=-1, keepdims=True)
            acc_scr[g] = alpha * acc_scr[g] + _mm(p.astype(jnp.bfloat16), v_c[:, sl])
            m_scr[g] = m_new

    attend(biasm_scr[...], km_ref[...], vm_ref[...])

    def attend_pair(i, carry):
        c = 2 * i
        rows = pl.ds(pl.multiple_of(c * KC, 2 * KC), 2 * KC)
        attend(jnp.concatenate([bias_scr[c], bias_scr[c + 1]], axis=1),
               k_ref[0, rows, :], v_ref[0, rows, :])
        return carry

    lax.fori_loop(0, n_seq // 2, attend_pair, 0)

    @pl.when(n_seq % 2 == 1)
    def _():
        c = n_seq - 1
        rows = pl.ds(pl.multiple_of(c * KC, KC), KC)
        attend(bias_scr[c], k_ref[0, rows, :], v_ref[0, rows, :])

    for g in range(N_KV_HEADS):
        o = acc_scr[g] / l_scr[g]
        for r in range(rep):
            hd = g * rep + r
            o_ref[0, :, hd * LANES:(hd + 1) * LANES] = o[r * QB:(r + 1) * QB].astype(o_ref.dtype)


def _cast_spec(w, nb, nq):
    rows, cols = w.shape
    if rows % (nb * nq * 16) == 0:
        tr = rows // (nb * nq)
        return pl.BlockSpec((tr, cols), lambda bi, qi_: (bi * nq + qi_, 0))
    assert rows % (nb * 16) == 0 and cols % (nq * LANES) == 0, w.shape
    return pl.BlockSpec((rows // nb, cols // nq), lambda bi, qi_: (bi, qi_))


def _attention(q, qi, wi, k, v, ki, km, vm, kim, k_sel, cast_weights):
    b, s, _ = q.shape
    nq = s // QB
    blk = lambda bi, qi_: (bi, qi_, 0)
    full = lambda bi, qi_: (bi, 0, 0)
    const = lambda bi, qi_: (0, 0)
    rep = N_HEADS // N_KV_HEADS
    f32 = jnp.float32
    cast_specs = [_cast_spec(w, b, nq) for w in cast_weights]
    return pl.pallas_call(
        functools.partial(_attn_kernel, k_sel, len(cast_weights)),
        out_shape=(jax.ShapeDtypeStruct((b, s, ATTN_WIDTH), jnp.bfloat16),
                   *[jax.ShapeDtypeStruct(w.shape, jnp.bfloat16) for w in cast_weights]),
        grid=(b, nq),
        in_specs=[
            pl.BlockSpec((1, QB, ATTN_WIDTH), blk),
            pl.BlockSpec((1, QB, IDX_WIDTH), blk),
            pl.BlockSpec((1, QB, LANES), blk),
            pl.BlockSpec((1, s, KV_WIDTH), full),
            pl.BlockSpec((1, s, KV_WIDTH), full),
            pl.BlockSpec((1, s, LANES), full),
            pl.BlockSpec((MW, KV_WIDTH), const),
            pl.BlockSpec((MW, KV_WIDTH), const),
            pl.BlockSpec((MW, LANES), const),
            *cast_specs,
        ],
        out_specs=(pl.BlockSpec((1, QB, ATTN_WIDTH), blk), *cast_specs),
        scratch_shapes=[
            pltpu.VMEM((N_KV_HEADS, rep * QB, LANES), jnp.bfloat16),
            pltpu.VMEM((N_IDX_HEADS // 4, 4 * QB, LANES), jnp.bfloat16),
            pltpu.VMEM((MW, QB), f32),
            pltpu.VMEM((s // KC, KC, QB), f32),
            pltpu.VMEM((QB, MW), f32),
            pltpu.VMEM((s // KC, QB, KC), f32),
            pltpu.VMEM((N_KV_HEADS, rep * QB, LANES), f32),
            pltpu.VMEM((N_KV_HEADS, rep * QB, LANES), f32),
            pltpu.VMEM((N_KV_HEADS, rep * QB, LANES), f32),
        ],
        compiler_params=_params(("parallel", "arbitrary")),
        name="dsa_attention",
    )(q, qi, wi, k, v, ki, km, vm, kim, *cast_weights)


def _merge_kernel(tiles_per_seq, h_ref, attn_ref, xc_ref, gb_ref, cg_ref, ga0_ref, ga1_ref,
                  gc0_ref, gc1_ref, xc_prev_ref, cg_prev_ref, xc_meta_ref, cg_meta_ref,
                  cw_ref, cb_ref, wa_ref, wc_ref, wo_ref, o_ref):
    f32 = jnp.float32
    first = pl.program_id(0) % tiles_per_seq == 0
    gx = cg_ref[...].astype(f32) * xc_ref[...].astype(f32)
    halo_prev = cg_prev_ref[...].astype(f32) * xc_prev_ref[...].astype(f32)
    halo_meta = cg_meta_ref[...].astype(f32) * xc_meta_ref[...].astype(f32)
    halo = jnp.where(first, halo_meta, halo_prev)
    h1, h2 = halo[HALO - 1:HALO], halo[HALO - 2:HALO - 1]
    row = lax.broadcasted_iota(jnp.int32, gx.shape, 0)
    prev1 = jnp.where(row == 0, h1, pltpu.roll(gx, 1, 0))
    prev2 = jnp.where(row == 0, h2, jnp.where(row == 1, h1, pltpu.roll(gx, 2, 0)))
    cw = cw_ref[...]
    conv = cw[0:1] * prev2 + cw[1:2] * prev1 + cw[2:3] * gx + cb_ref[...]
    feat = (gb_ref[...].astype(f32) * conv).astype(jnp.bfloat16)
    y_conv = _mm(feat, wc_ref[...])
    y_attn = _mm(attn_ref[...], wa_ref[...])
    half = D_MODEL // 2
    sig = lambda r: jax.nn.sigmoid(r[...].astype(f32))
    merged = jnp.concatenate(
        [sig(ga0_ref) * y_attn[:, :half] + sig(gc0_ref) * y_conv[:, :half],
         sig(ga1_ref) * y_attn[:, half:] + sig(gc1_ref) * y_conv[:, half:]], axis=1)
    o_ref[...] = h_ref[...] + _mm(merged.astype(jnp.bfloat16), wo_ref[...])


def _merge(h, attn, zg, zg_meta, conv_w, conv_b, wa, wc, wo, seq):
    n = h.shape[0]
    tm = MERGE_TM
    tiles_per_seq = seq // tm
    row2 = lambda i: (i, 0)
    const = lambda i: (0, 0)
    col = lambda c: pl.BlockSpec((tm, CONV_WIDTH), lambda i: (i, c))
    prev = lambda c: pl.BlockSpec((HALO, CONV_WIDTH),
                                  lambda i: (jnp.maximum(i * (tm // HALO) - 1, 0), c))
    meta = lambda c: pl.BlockSpec((HALO, CONV_WIDTH), lambda i: (N_META // HALO - 1, c))
    single = pl.Buffered(1)
    return pl.pallas_call(
        functools.partial(_merge_kernel, tiles_per_seq),
        out_shape=jax.ShapeDtypeStruct((n, D_MODEL), jnp.float32),
        grid=(n // tm,),
        in_specs=[
            pl.BlockSpec((tm, D_MODEL), row2),
            pl.BlockSpec((tm, ATTN_WIDTH), row2),
            col(0), col(1), col(2), col(3), col(4), col(5), col(6),
            prev(0), prev(2), meta(0), meta(2),
            pl.BlockSpec((CONV_K, CONV_WIDTH), const),
            pl.BlockSpec((1, CONV_WIDTH), const),
            pl.BlockSpec((ATTN_WIDTH, D_MODEL), const, pipeline_mode=single),
            pl.BlockSpec((CONV_WIDTH, D_MODEL), const, pipeline_mode=single),
            pl.BlockSpec((D_MODEL, D_MODEL), const, pipeline_mode=single),
        ],
        out_specs=pl.BlockSpec((tm, D_MODEL), row2),
        compiler_params=_params(("parallel",)),
        name="merge",
    )(h, attn, zg, zg, zg, zg, zg, zg, zg, zg, zg, zg_meta, zg_meta, conv_w, conv_b, wa, wc, wo)


def _rope_tables(pos, head_dim):
    rot = head_dim // ROT_DIV
    half = rot // 2
    inv = ROPE_THETA ** (-jnp.arange(0, rot, 2, dtype=jnp.float32) / rot)
    ang = pos.astype(jnp.float32)[:, None] * inv[None, :]
    cos, sin = jnp.cos(ang), jnp.sin(ang)
    n = pos.shape[0]
    pad = head_dim - rot
    c = jnp.concatenate([cos, cos, jnp.ones((n, pad), jnp.float32)], axis=1)
    zeros_h = jnp.zeros((n, half), jnp.float32)
    zeros_p = jnp.zeros((n, pad), jnp.float32)
    s_lo = jnp.concatenate([-sin, zeros_h, zeros_p], axis=1)
    s_hi = jnp.concatenate([zeros_h, sin, zeros_p], axis=1)
    tile = LANES // head_dim
    return [jnp.tile(t, (1, tile)) for t in (c, s_lo, s_hi)]


def kernel(x, meta_tokens, ffn1_norm_g, ffn1_w_gate, ffn1_w_up, ffn1_w_down, mix_norm_g, w_in, q_norm_g, k_norm_g, conv_w, conv_b, w_attn_branch, w_conv_branch, w_out, ffn2_norm_g, ffn2_w_gate, ffn2_w_up, ffn2_w_down):
    bsz, seq, d = x.shape
    bf = jnp.bfloat16
    k_sel = min(TOPK_MAX, seq // 4)
    assert d == D_MODEL and seq % QB == 0 and k_sel <= KC

    widths = [ATTN_WIDTH, KV_WIDTH, KV_WIDTH, IDX_WIDTH, IDX_DIM, N_IDX_HEADS,
              CONV_WIDTH, CONV_WIDTH, CONV_WIDTH, D_MODEL, D_MODEL]
    offs = [int(o) for o in np.concatenate([[0], np.cumsum(widths)])]
    seg = lambda w, i, j=None: w[:, offs[i]:offs[i + 1 if j is None else j]].astype(bf)

    h = x.reshape(bsz * seq, d)
    hm = meta_tokens.astype(x.dtype)
    pos_seq = jnp.arange(N_META, N_META + seq, dtype=jnp.int32)
    pos_meta = jnp.arange(N_META, dtype=jnp.int32)
    tabs_seq = jnp.stack(_rope_tables(pos_seq, HEAD_DIM) + _rope_tables(pos_seq, IDX_DIM))
    tabs_meta = jnp.stack(_rope_tables(pos_meta, HEAD_DIM) + _rope_tables(pos_meta, IDX_DIM))

    for l in range(ffn1_norm_g.shape[0]):
        wl = w_in[l]
        w_attn = (seg(wl, 0), seg(wl, 3), seg(wl, 1), seg(wl, 2),
                  jnp.pad(seg(wl, 4, 6), ((0, 0), (0, LANES - IDX_DIM - N_IDX_HEADS))))
        w_g = seg(wl, 6, 11)
        g1 = ffn1_norm_g[l][None]
        gm = mix_norm_g[l][None]

        hm, wg1, wu1, wd1 = _ffn(hm, g1, ffn1_w_gate[l], ffn1_w_up[l], ffn1_w_down[l],
                                 emit_weights=True)
        h, = _ffn(h, g1, wg1, wu1, wd1)

        gq, gk = q_norm_g[l][None], k_norm_g[l][None]
        q, qi, k, v, ki, wi = _proj_attn(h, gm, *w_attn, gq, gk, tabs_seq)
        _, _, km, vm, kim, _ = _proj_attn(hm, gm, *w_attn, gq, gk, tabs_meta)
        zg = _proj_gate(h, gm, w_g)
        zg_meta = _proj_gate(hm, gm, w_g)

        pad_rows = lambda a: jnp.pad(a, ((0, MW - N_META), (0, 0)))
        r3 = lambda a: a.reshape(bsz, seq, a.shape[-1])
        later = [w_attn_branch[l], w_conv_branch[l], w_out[l],
                 ffn2_w_gate[l], ffn2_w_up[l], ffn2_w_down[l]]
        attn, wa, wc, wo, wg2, wu2, wd2 = _attention(
            r3(q), r3(qi), r3(wi), r3(k), r3(v), r3(ki),
            pad_rows(km), pad_rows(vm), pad_rows(kim), k_sel, later)
        attn = attn.reshape(bsz * seq, ATTN_WIDTH)

        h = _merge(h, attn, zg, zg_meta, conv_w[l], conv_b[l][None], wa, wc, wo, seq)

        h, = _ffn(h, ffn2_norm_g[l][None], wg2, wu2, wd2)

    return h.reshape(bsz, seq, d)
```

```python
import functools
import math

import jax
import jax.numpy as jnp
import numpy as np
from jax import lax
from jax.experimental import pallas as pl
from jax.experimental.pallas import tpu as pltpu

D_MODEL = 2048
N_META = 16
N_HEADS = 8
N_KV_HEADS = 2
HEAD_DIM = 128
ATTN_WIDTH = N_HEADS * HEAD_DIM
KV_WIDTH = N_KV_HEADS * HEAD_DIM
N_IDX_HEADS = 16
IDX_DIM = 64
IDX_WIDTH = N_IDX_HEADS * IDX_DIM
TOPK_MAX = 256
CONV_WIDTH = D_MODEL // 2
CONV_K = 3
D_FF = 5632
ROPE_THETA = 500000.0
ROT_DIV = 4
EPS = 1e-6

LANES = 128
VMEM_LIMIT = 56 * 1024 * 1024
NEG_BIG = -1e30
Q_SCALE = math.log2(math.e) / math.sqrt(HEAD_DIM)

FFN_TM, FFN_TF = 1024, 512
PROJ_TM = 256
GATE_TM, GATE_TN = 1024, 1792
MERGE_TM = 256
QB = 256
KC = 256
MW = 128
BISECT_FIXED = 15
BISECT_STEPS = 3
HALO = 16


def _params(sem):
    return pltpu.CompilerParams(dimension_semantics=sem, vmem_limit_bytes=VMEM_LIMIT)


def _rms(x, g):
    ms = jnp.mean(x * x, axis=-1, keepdims=True)
    return x * lax.rsqrt(ms + EPS) * g


def _mm(a, b):
    return jnp.dot(a, b, preferred_element_type=jnp.float32)


def _ffn_kernel(emit_weights, h_ref, g_ref, wg_ref, wu_ref, wd_ref, o_ref, *rest):
    u_scr = rest[-1]
    j = pl.program_id(1)

    @pl.when(j == 0)
    def _():
        h = h_ref[...]
        u_scr[...] = _rms(h, g_ref[...]).astype(jnp.bfloat16)
        o_ref[...] = h

    bf = jnp.bfloat16
    wg, wu, wd = wg_ref[...].astype(bf), wu_ref[...].astype(bf), wd_ref[...].astype(bf)
    if emit_weights:
        for dst, w in zip(rest[:3], (wg, wu, wd)):
            dst[...] = w
    u = u_scr[...]
    gate = _mm(u, wg)
    up = _mm(u, wu)
    a = (gate * jax.nn.sigmoid(gate)) * (up * 0.5)
    o_ref[...] += _mm(a.astype(bf), wd)


def _ffn(h, g, wg, wu, wd, emit_weights=False):
    n = h.shape[0]
    tm = min(FFN_TM, n)
    assert not emit_weights or n == tm
    w_specs = [
        pl.BlockSpec((D_MODEL, FFN_TF), lambda i, j: (0, j)),
        pl.BlockSpec((D_MODEL, FFN_TF), lambda i, j: (0, j)),
        pl.BlockSpec((FFN_TF, D_MODEL), lambda i, j: (j, 0)),
    ]
    out_shape = [jax.ShapeDtypeStruct((n, D_MODEL), jnp.float32)]
    out_specs = [pl.BlockSpec((tm, D_MODEL), lambda i, j: (i, 0))]
    if emit_weights:
        out_shape += [jax.ShapeDtypeStruct(w.shape, jnp.bfloat16) for w in (wg, wu, wd)]
        out_specs += w_specs
    return pl.pallas_call(
        functools.partial(_ffn_kernel, emit_weights),
        out_shape=out_shape,
        grid=(n // tm, D_FF // FFN_TF),
        in_specs=[
            pl.BlockSpec((tm, D_MODEL), lambda i, j: (i, 0)),
            pl.BlockSpec((1, D_MODEL), lambda i, j: (0, 0)),
            *w_specs,
        ],
        out_specs=out_specs,
        scratch_shapes=[pltpu.VMEM((tm, D_MODEL), jnp.bfloat16)],
        compiler_params=_params(("parallel", "arbitrary")),
        name="ffn",
    )(h, g, wg, wu, wd)


def _rope(x, cos, s_lo, s_hi, half):
    n = x.shape[-1]
    return x * cos + pltpu.roll(x, n - half, 1) * s_lo + pltpu.roll(x, half, 1) * s_hi


def _proj_attn_kernel(h_ref, g_ref, wq_ref, wqi_ref, wk_ref, wv_ref, wkw_ref, gq_ref, gk_ref,
                      tab_ref, q_ref, qi_ref, k_ref, v_ref, ki_ref, wi_ref):
    bf = jnp.bfloat16
    u = _rms(h_ref[...], g_ref[...]).astype(bf)
    ca, sa_lo, sa_hi = tab_ref[0], tab_ref[1], tab_ref[2]
    ci, si_lo, si_hi = tab_ref[3], tab_ref[4], tab_ref[5]
    half_a = HEAD_DIM // ROT_DIV // 2
    half_i = IDX_DIM // ROT_DIV // 2
    gq = gq_ref[...]
    gk = gk_ref[...]
    z = _mm(u, wq_ref[...])
    for hd in range(N_HEADS):
        sl = slice(hd * LANES, (hd + 1) * LANES)
        q = _rope(_rms(z[:, sl], gq), ca, sa_lo, sa_hi, half_a)
        q_ref[:, sl] = (q * Q_SCALE).astype(bf)
    z = _mm(u, wqi_ref[...])
    for p in range(IDX_WIDTH // LANES):
        sl = slice(p * LANES, (p + 1) * LANES)
        qi_ref[:, sl] = _rope(z[:, sl], ci, si_lo, si_hi, half_i).astype(bf)
    z = _mm(u, wk_ref[...])
    for hd in range(N_KV_HEADS):
        sl = slice(hd * LANES, (hd + 1) * LANES)
        k_ref[:, sl] = _rope(_rms(z[:, sl], gk), ca, sa_lo, sa_hi, half_a).astype(bf)
    v_ref[...] = _mm(u, wv_ref[...]).astype(bf)
    x = _mm(u, wkw_ref[...])
    lane = lax.broadcasted_iota(jnp.int32, x.shape, 1)
    is_ki = lane < IDX_DIM
    roped = _rope(x, jnp.where(is_ki, ci, 1.0), jnp.where(is_ki, si_lo, 0.0),
                  jnp.where(is_ki, si_hi, 0.0), half_i)
    swapped = pltpu.roll(roped, IDX_DIM, 1)
    ki_ref[...] = jnp.where(is_ki, roped, swapped).astype(bf)
    wi_ref[...] = swapped


def _proj_attn(h, g, wq, wqi, wk, wv, wkw, gq, gk, tabs):
    n = h.shape[0]
    tm = min(PROJ_TM, n)
    nt = tabs.shape[1] // tm
    row = lambda i: (i, 0)
    const = lambda i: (0, 0)
    bf = jnp.bfloat16
    weight = lambda w: pl.BlockSpec(w.shape, const)
    return pl.pallas_call(
        _proj_attn_kernel,
        out_shape=(
            jax.ShapeDtypeStruct((n, ATTN_WIDTH), bf),
            jax.ShapeDtypeStruct((n, IDX_WIDTH), bf),
            jax.ShapeDtypeStruct((n, KV_WIDTH), bf),
            jax.ShapeDtypeStruct((n, KV_WIDTH), bf),
            jax.ShapeDtypeStruct((n, LANES), bf),
            jax.ShapeDtypeStruct((n, LANES), jnp.float32),
        ),
        grid=(n // tm,),
        in_specs=[
            pl.BlockSpec((tm, D_MODEL), row),
            pl.BlockSpec((1, D_MODEL), const),
            weight(wq), weight(wqi), weight(wk), weight(wv), weight(wkw),
            pl.BlockSpec((1, LANES), const),
            pl.BlockSpec((1, LANES), const),
            pl.BlockSpec((6, tm, LANES), lambda i: (0, i % nt, 0)),
        ],
        out_specs=(
            pl.BlockSpec((tm, ATTN_WIDTH), row),
            pl.BlockSpec((tm, IDX_WIDTH), row),
            pl.BlockSpec((tm, KV_WIDTH), row),
            pl.BlockSpec((tm, KV_WIDTH), row),
            pl.BlockSpec((tm, LANES), row),
            pl.BlockSpec((tm, LANES), row),
        ),
        compiler_params=_params(("parallel",)),
        name="proj_attn",
    )(h, g, wq, wqi, wk, wv, wkw, gq, gk, tabs)


def _proj_gate_kernel(h_ref, g_ref, w_ref, o_ref, u_scr):
    @pl.when(pl.program_id(1) == 0)
    def _():
        u_scr[...] = _rms(h_ref[...], g_ref[...]).astype(jnp.bfloat16)

    o_ref[...] = _mm(u_scr[...], w_ref[...]).astype(o_ref.dtype)


def _proj_gate(h, g, w):
    n = h.shape[0]
    tm = min(GATE_TM, n)
    cols = w.shape[1]
    return pl.pallas_call(
        _proj_gate_kernel,
        out_shape=jax.ShapeDtypeStruct((n, cols), jnp.bfloat16),
        grid=(n // tm, cols // GATE_TN),
        in_specs=[
            pl.BlockSpec((tm, D_MODEL), lambda i, j: (i, 0)),
            pl.BlockSpec((1, D_MODEL), lambda i, j: (0, 0)),
            pl.BlockSpec((D_MODEL, GATE_TN), lambda i, j: (0, j)),
        ],
        out_specs=pl.BlockSpec((tm, GATE_TN), lambda i, j: (i, j)),
        scratch_shapes=[pltpu.VMEM((tm, D_MODEL), jnp.bfloat16)],
        compiler_params=_params(("parallel", "arbitrary")),
        name="proj_gate",
    )(h, g, w)


def _dot_nt(a, b):
    return lax.dot_general(a, b, (((1,), (1,)), ((), ())), preferred_element_type=jnp.float32)


def _attn_kernel(k_sel, n_cast, q_ref, qi_ref, wi_ref, k_ref, v_ref, ki_ref, km_ref, vm_ref,
                 kim_ref, *rest):
    cast_in, o_ref, cast_out = rest[:n_cast], rest[n_cast], rest[n_cast + 1:2 * n_cast + 1]
    (qs_scr, qis_scr, sctm_scr, sct_scr, biasm_scr, bias_scr,
     m_scr, l_scr, acc_scr) = rest[2 * n_cast + 1:]
    for src, dst in zip(cast_in, cast_out):
        dst[...] = src[...].astype(dst.dtype)

    qb = pl.program_id(1)
    n_seq = qb + 1
    rep = N_HEADS // N_KV_HEADS
    hpg = 4
    lane_q = lax.broadcasted_iota(jnp.int32, (QB, LANES), 1)
    neg_inf = float("-inf")

    for g in range(N_KV_HEADS):
        for r in range(rep):
            hd = g * rep + r
            qs_scr[g, r * QB:(r + 1) * QB, :] = q_ref[0, :, hd * LANES:(hd + 1) * LANES]
    for i in range(N_IDX_HEADS // hpg):
        for r in range(hpg):
            hd = i * hpg + r
            pair = qi_ref[0, :, (hd // 2) * LANES:(hd // 2 + 1) * LANES]
            keep = (lane_q < IDX_DIM) if hd % 2 == 0 else (lane_q >= IDX_DIM)
            qis_scr[i, r * QB:(r + 1) * QB, :] = jnp.where(keep, pair, jnp.zeros_like(pair))

    w_all = wi_ref[0]

    def index_scores(ki_c):
        kw = ki_c.shape[0]
        score = jnp.zeros((QB, kw), jnp.float32)
        for i in range(N_IDX_HEADS // hpg):
            d = _dot_nt(qis_scr[i], ki_c)
            for r in range(hpg):
                hd = i * hpg + r
                score = score + jnp.maximum(d[r * QB:(r + 1) * QB], 0.0) * w_all[:, hd:hd + 1]
        return score

    lane_m = lax.broadcasted_iota(jnp.int32, (QB, MW), 1)
    sctm_scr[...] = jnp.where(lane_m < N_META, index_scores(kim_ref[...]), neg_inf).T
    row = lax.broadcasted_iota(jnp.int32, (QB, KC), 0)
    lane = lax.broadcasted_iota(jnp.int32, (QB, KC), 1)

    def score_chunk(c):
        ki_c = ki_ref[0, pl.ds(pl.multiple_of(c * KC, KC), KC), :]
        causal = (c - qb) * KC + lane <= row
        s = jnp.where(causal, index_scores(ki_c), neg_inf)
        bias_scr[c] = s
        sct_scr[c] = s.T

    def score_pair(i, carry):
        score_chunk(2 * i)
        score_chunk(2 * i + 1)
        return carry

    lax.fori_loop(0, n_seq // 2, score_pair, 0)

    @pl.when(n_seq % 2 == 1)
    def _():
        score_chunk(n_seq - 1)
        sct_scr[n_seq] = jnp.full((KC, QB), neg_inf, jnp.float32)

    def over_keys(f, init):
        def pair(i, a):
            c = 2 * i
            a = f(sct_scr[c], MW + c * KC, a)
            return f(sct_scr[c + 1], MW + (c + 1) * KC, a)
        return lax.fori_loop(0, (n_seq + 1) // 2, pair, f(sctm_scr[...], 0, init))

    def fold8(x):
        return jnp.sum(x.reshape(x.shape[0] // 8, 8, QB), axis=0)

    def count(pred):
        part = over_keys(lambda x, p0, a: a + fold8(jnp.where(pred(x, p0), 1.0, 0.0)),
                         jnp.zeros((8, QB), jnp.float32))
        return jnp.sum(part, axis=0, keepdims=True)

    kf = float(k_sel)
    qlane = lax.broadcasted_iota(jnp.int32, (1, QB), 1)
    n_valid = (N_META + 1 + qb * QB + qlane).astype(jnp.float32)

    def minmax(x, p0, carry):
        mn, mx = carry
        x3 = x.reshape(x.shape[0] // 8, 8, QB)
        mx = jnp.maximum(mx, jnp.max(x3, axis=0))
        mn = jnp.minimum(mn, jnp.min(jnp.where(x3 == neg_inf, float("inf"), x3), axis=0))
        return mn, mx

    mn, mx = over_keys(minmax, (jnp.full((8, QB), float("inf"), jnp.float32),
                                jnp.full((8, QB), neg_inf, jnp.float32)))
    mn = jnp.min(mn, axis=0, keepdims=True)
    mx = jnp.max(mx, axis=0, keepdims=True)

    def count_ge(t):
        return count(lambda x, p0: x >= t)

    def midpoint(lo, hi):
        return lo + (hi - lo) * 0.5

    def is_active(lo, hi, cnt_lo):
        mid = midpoint(lo, hi)
        return jnp.where((cnt_lo != kf) & (mid > lo) & (mid < hi), 1.0, 0.0)

    cnt_mx = count_ge(mx)
    at_max = cnt_mx >= kf
    lo0 = jnp.where(at_max, mx, mn)
    cnt0 = jnp.where(at_max, cnt_mx, n_valid)
    act0 = jnp.where(n_valid > kf, is_active(lo0, mx, cnt0), 0.0)

    def bisect_once(state):
        lo, hi, cnt_lo, act = state
        mid = midpoint(lo, hi)
        cnt = count_ge(mid)
        up = (act > 0.5) & (cnt >= kf)
        down = (act > 0.5) & (cnt < kf)
        lo = jnp.where(up, mid, lo)
        cnt_lo = jnp.where(up, cnt, cnt_lo)
        hi = jnp.where(down, mid, hi)
        return lo, hi, cnt_lo, act * is_active(lo, hi, cnt_lo)

    def bisect_trip(st):
        for _ in range(BISECT_STEPS):
            st = bisect_once(st)
        return st

    st = lax.fori_loop(0, BISECT_FIXED, lambda _, st: bisect_once(st), (lo0, mx, cnt0, act0))
    thr, _, cnt_thr, _ = lax.while_loop(lambda st: jnp.max(st[3]) > 0.5, bisect_trip, st)
    has_tie = cnt_thr > kf
    any_tie = jnp.max(jnp.where(has_tie, 1.0, 0.0)) > 0.5

    def write_bias(keep):
        biasm_scr[...] = jnp.where(keep(sctm_scr[...], 0), 0.0, NEG_BIG).T

        def body(c, carry):
            bias_scr[c] = jnp.where(keep(sct_scr[c], MW + c * KC), 0.0, NEG_BIG).T
            return carry
        lax.fori_loop(0, n_seq, body, 0)

    @pl.when(jnp.logical_not(any_tie))
    def _():
        biasm_scr[...] = jnp.where(sctm_scr[...] >= thr, 0.0, NEG_BIG).T
        thr_q = jnp.broadcast_to(thr, (LANES, QB)).T
        thr_q = jnp.concatenate([thr_q] * (KC // LANES), axis=1)

        def body(c, carry):
            bias_scr[c] = jnp.where(bias_scr[c] >= thr_q, 0.0, NEG_BIG)
            return carry
        lax.fori_loop(0, n_seq, body, 0)

    @pl.when(any_tie)
    def _():
        far = 2 ** 30

        def tie_pos(x, p0):
            pos = p0 + lax.broadcasted_iota(jnp.int32, x.shape, 0)
            return jnp.where(x == thr, pos, far)

        need = kf - count(lambda x, p0: x > thr)

        def pos_step(i, last):
            cand = last | lax.shift_left(jnp.int32(1), 11 - i)
            below = count(lambda x, p0: tie_pos(x, p0) < cand)
            return jnp.where(below < need, cand, last)

        last = lax.fori_loop(0, 12, pos_step, jnp.zeros((1, QB), jnp.int32))
        last = jnp.where(has_tie, last, far - 1)
        write_bias(lambda x, p0: jnp.where(x > thr, 1.0,
                                           jnp.where(tie_pos(x, p0) <= last, 1.0, 0.0)) > 0.5)

    m_scr[...] = jnp.full(m_scr.shape, NEG_BIG, jnp.float32)
    l_scr[...] = jnp.zeros(l_scr.shape, jnp.float32)
    acc_scr[...] = jnp.zeros(acc_scr.shape, jnp.float32)

    def attend(bias, k_c, v_c):
        kw = k_c.shape[0]
        for g in range(N_KV_HEADS):
            sl = slice(g * LANES, (g + 1) * LANES)
            s = _dot_nt(qs_scr[g], k_c[:, sl])
            s = jnp.concatenate([s[r * QB:(r + 1) * QB] + bias for r in range(rep)], axis=0)
            m_prev = m_scr[g]
            m_new = jnp.maximum(m_prev, jnp.max(s, axis=-1, keepdims=True))
            alpha = jnp.exp2(m_prev - m_new)
            p = jnp.exp2(s - jnp.concatenate([m_new] * (kw // LANES), axis=1))
            l_scr[g] = alpha * l_scr[g] + jnp.sum(p, axis=-1, keepdims=True)
            acc_scr[g] = alpha * acc_scr[g] + _mm(p.astype(jnp.bfloat16), v_c[:, sl])
            m_scr[g] = m_new

    attend(biasm_scr[...], km_ref[...], vm_ref[...])

    def attend_pair(i, carry):
        c = 2 * i
        rows = pl.ds(pl.multiple_of(c * KC, 2 * KC), 2 * KC)
        attend(jnp.concatenate([bias_scr[c], bias_scr[c + 1]], axis=1),
               k_ref[0, rows, :], v_ref[0, rows, :])
        return carry

    lax.fori_loop(0, n_seq // 2, attend_pair, 0)

    @pl.when(n_seq % 2 == 1)
    def _():
        c = n_seq - 1
        rows = pl.ds(pl.multiple_of(c * KC, KC), KC)
        attend(bias_scr[c], k_ref[0, rows, :], v_ref[0, rows, :])

    for g in range(N_KV_HEADS):
        o = acc_scr[g] / l_scr[g]
        for r in range(rep):
            hd = g * rep + r
            o_ref[0, :, hd * LANES:(hd + 1) * LANES] = o[r * QB:(r + 1) * QB].astype(o_ref.dtype)


def _cast_spec(w, nb, nq):
    rows, cols = w.shape
    if rows % (nb * nq * 16) == 0:
        tr = rows // (nb * nq)
        return pl.BlockSpec((tr, cols), lambda bi, qi_: (bi * nq + qi_, 0))
    assert rows % (nb * 16) == 0 and cols % (nq * LANES) == 0, w.shape
    return pl.BlockSpec((rows // nb, cols // nq), lambda bi, qi_: (bi, qi_))


def _attention(q, qi, wi, k, v, ki, km, vm, kim, k_sel, cast_weights):
    b, s, _ = q.shape
    nq = s // QB
    blk = lambda bi, qi_: (bi, qi_, 0)
    full = lambda bi, qi_: (bi, 0, 0)
    const = lambda bi, qi_: (0, 0)
    rep = N_HEADS // N_KV_HEADS
    f32 = jnp.float32
    cast_specs = [_cast_spec(w, b, nq) for w in cast_weights]
    return pl.pallas_call(
        functools.partial(_attn_kernel, k_sel, len(cast_weights)),
        out_shape=(jax.ShapeDtypeStruct((b, s, ATTN_WIDTH), jnp.bfloat16),
                   *[jax.ShapeDtypeStruct(w.shape, jnp.bfloat16) for w in cast_weights]),
        grid=(b, nq),
        in_specs=[
            pl.BlockSpec((1, QB, ATTN_WIDTH), blk),
            pl.BlockSpec((1, QB, IDX_WIDTH), blk),
            pl.BlockSpec((1, QB, LANES), blk),
            pl.BlockSpec((1, s, KV_WIDTH), full),
            pl.BlockSpec((1, s, KV_WIDTH), full),
            pl.BlockSpec((1, s, LANES), full),
            pl.BlockSpec((MW, KV_WIDTH), const),
            pl.BlockSpec((MW, KV_WIDTH), const),
            pl.BlockSpec((MW, LANES), const),
            *cast_specs,
        ],
        out_specs=(pl.BlockSpec((1, QB, ATTN_WIDTH), blk), *cast_specs),
        scratch_shapes=[
            pltpu.VMEM((N_KV_HEADS, rep * QB, LANES), jnp.bfloat16),
            pltpu.VMEM((N_IDX_HEADS // 4, 4 * QB, LANES), jnp.bfloat16),
            pltpu.VMEM((MW, QB), f32),
            pltpu.VMEM((s // KC, KC, QB), f32),
            pltpu.VMEM((QB, MW), f32),
            pltpu.VMEM((s // KC, QB, KC), f32),
            pltpu.VMEM((N_KV_HEADS, rep * QB, LANES), f32),
            pltpu.VMEM((N_KV_HEADS, rep * QB, LANES), f32),
            pltpu.VMEM((N_KV_HEADS, rep * QB, LANES), f32),
        ],
        compiler_params=_params(("parallel", "arbitrary")),
        name="dsa_attention",
    )(q, qi, wi, k, v, ki, km, vm, kim, *cast_weights)


def _merge_kernel(tiles_per_seq, h_ref, attn_ref, xc_ref, gb_ref, cg_ref, ga0_ref, ga1_ref,
                  gc0_ref, gc1_ref, xc_prev_ref, cg_prev_ref, xc_meta_ref, cg_meta_ref,
                  cw_ref, cb_ref, wa_ref, wc_ref, wo_ref, o_ref):
    f32 = jnp.float32
    first = pl.program_id(0) % tiles_per_seq == 0
    gx = cg_ref[...].astype(f32) * xc_ref[...].astype(f32)
    halo_prev = cg_prev_ref[...].astype(f32) * xc_prev_ref[...].astype(f32)
    halo_meta = cg_meta_ref[...].astype(f32) * xc_meta_ref[...].astype(f32)
    halo = jnp.where(first, halo_meta, halo_prev)
    h1, h2 = halo[HALO - 1:HALO], halo[HALO - 2:HALO - 1]
    row = lax.broadcasted_iota(jnp.int32, gx.shape, 0)
    prev1 = jnp.where(row == 0, h1, pltpu.roll(gx, 1, 0))
    prev2 = jnp.where(row == 0, h2, jnp.where(row == 1, h1, pltpu.roll(gx, 2, 0)))
    cw = cw_ref[...]
    conv = cw[0:1] * prev2 + cw[1:2] * prev1 + cw[2:3] * gx + cb_ref[...]
    feat = (gb_ref[...].astype(f32) * conv).astype(jnp.bfloat16)
    y_conv = _mm(feat, wc_ref[...])
    y_attn = _mm(attn_ref[...], wa_ref[...])
    half = D_MODEL // 2
    sig = lambda r: jax.nn.sigmoid(r[...].astype(f32))
    merged = jnp.concatenate(
        [sig(ga0_ref) * y_attn[:, :half] + sig(gc0_ref) * y_conv[:, :half],
         sig(ga1_ref) * y_attn[:, half:] + sig(gc1_ref) * y_conv[:, half:]], axis=1)
    o_ref[...] = h_ref[...] + _mm(merged.astype(jnp.bfloat16), wo_ref[...])


def _merge(h, attn, zg, zg_meta, conv_w, conv_b, wa, wc, wo, seq):
    n = h.shape[0]
    tm = MERGE_TM
    tiles_per_seq = seq // tm
    row2 = lambda i: (i, 0)
    const = lambda i: (0, 0)
    col = lambda c: pl.BlockSpec((tm, CONV_WIDTH), lambda i: (i, c))
    prev = lambda c: pl.BlockSpec((HALO, CONV_WIDTH),
                                  lambda i: (jnp.maximum(i * (tm // HALO) - 1, 0), c))
    meta = lambda c: pl.BlockSpec((HALO, CONV_WIDTH), lambda i: (N_META // HALO - 1, c))
    single = pl.Buffered(1)
    return pl.pallas_call(
        functools.partial(_merge_kernel, tiles_per_seq),
        out_shape=jax.ShapeDtypeStruct((n, D_MODEL), jnp.float32),
        grid=(n // tm,),
        in_specs=[
            pl.BlockSpec((tm, D_MODEL), row2),
            pl.BlockSpec((tm, ATTN_WIDTH), row2),
            col(0), col(1), col(2), col(3), col(4), col(5), col(6),
            prev(0), prev(2), meta(0), meta(2),
            pl.BlockSpec((CONV_K, CONV_WIDTH), const),
            pl.BlockSpec((1, CONV_WIDTH), const),
            pl.BlockSpec((ATTN_WIDTH, D_MODEL), const, pipeline_mode=single),
            pl.BlockSpec((CONV_WIDTH, D_MODEL), const, pipeline_mode=single),
            pl.BlockSpec((D_MODEL, D_MODEL), const, pipeline_mode=single),
        ],
        out_specs=pl.BlockSpec((tm, D_MODEL), row2),
        compiler_params=_params(("parallel",)),
        name="merge",
    )(h, attn, zg, zg, zg, zg, zg, zg, zg, zg, zg, zg_meta, zg_meta, conv_w, conv_b, wa, wc, wo)


def _rope_tables(pos, head_dim):
    rot = head_dim // ROT_DIV
    half = rot // 2
    inv = ROPE_THETA ** (-jnp.arange(0, rot, 2, dtype=jnp.float32) / rot)
    ang = pos.astype(jnp.float32)[:, None] * inv[None, :]
    cos, sin = jnp.cos(ang), jnp.sin(ang)
    n = pos.shape[0]
    pad = head_dim - rot
    c = jnp.concatenate([cos, cos, jnp.ones((n, pad), jnp.float32)], axis=1)
    zeros_h = jnp.zeros((n, half), jnp.float32)
    zeros_p = jnp.zeros((n, pad), jnp.float32)
    s_lo = jnp.concatenate([-sin, zeros_h, zeros_p], axis=1)
    s_hi = jnp.concatenate([zeros_h, sin, zeros_p], axis=1)
    tile = LANES // head_dim
    return [jnp.tile(t, (1, tile)) for t in (c, s_lo, s_hi)]


def kernel(x, meta_tokens, ffn1_norm_g, ffn1_w_gate, ffn1_w_up, ffn1_w_down, mix_norm_g, w_in, q_norm_g, k_norm_g, conv_w, conv_b, w_attn_branch, w_conv_branch, w_out, ffn2_norm_g, ffn2_w_gate, ffn2_w_up, ffn2_w_down):
    bsz, seq, d = x.shape
    bf = jnp.bfloat16
    k_sel = min(TOPK_MAX, seq // 4)
    assert d == D_MODEL and seq % QB == 0 and k_sel <= KC

    widths = [ATTN_WIDTH, KV_WIDTH, KV_WIDTH, IDX_WIDTH, IDX_DIM, N_IDX_HEADS,
              CONV_WIDTH, CONV_WIDTH, CONV_WIDTH, D_MODEL, D_MODEL]
    offs = [int(o) for o in np.concatenate([[0], np.cumsum(widths)])]
    seg = lambda w, i, j=None: w[:, offs[i]:offs[i + 1 if j is None else j]].astype(bf)

    h = x.reshape(bsz * seq, d)
    hm = meta_tokens.astype(x.dtype)
    pos_seq = jnp.arange(N_META, N_META + seq, dtype=jnp.int32)
    pos_meta = jnp.arange(N_META, dtype=jnp.int32)
    tabs_seq = jnp.stack(_rope_tables(pos_seq, HEAD_DIM) + _rope_tables(pos_seq, IDX_DIM))
    tabs_meta = jnp.stack(_rope_tables(pos_meta, HEAD_DIM) + _rope_tables(pos_meta, IDX_DIM))

    for l in range(ffn1_norm_g.shape[0]):
        wl = w_in[l]
        w_attn = (seg(wl, 0), seg(wl, 3), seg(wl, 1), seg(wl, 2),
                  jnp.pad(seg(wl, 4, 6), ((0, 0), (0, LANES - IDX_DIM - N_IDX_HEADS))))
        w_g = seg(wl, 6, 11)
        g1 = ffn1_norm_g[l][None]
        gm = mix_norm_g[l][None]

        hm, wg1, wu1, wd1 = _ffn(hm, g1, ffn1_w_gate[l], ffn1_w_up[l], ffn1_w_down[l],
                                 emit_weights=True)
        h, = _ffn(h, g1, wg1, wu1, wd1)

        gq, gk = q_norm_g[l][None], k_norm_g[l][None]
        q, qi, k, v, ki, wi = _proj_attn(h, gm, *w_attn, gq, gk, tabs_seq)
        _, _, km, vm, kim, _ = _proj_attn(hm, gm, *w_attn, gq, gk, tabs_meta)
        zg = _proj_gate(h, gm, w_g)
        zg_meta = _proj_gate(hm, gm, w_g)

        pad_rows = lambda a: jnp.pad(a, ((0, MW - N_META), (0, 0)))
        r3 = lambda a: a.reshape(bsz, seq, a.shape[-1])
        later = [w_attn_branch[l], w_conv_branch[l], w_out[l],
                 ffn2_w_gate[l], ffn2_w_up[l], ffn2_w_down[l]]
        attn, wa, wc, wo, wg2, wu2, wd2 = _attention(
            r3(q), r3(qi), r3(wi), r3(k), r3(v), r3(ki),
            pad_rows(km), pad_rows(vm), pad_rows(kim), k_sel, later)
        attn = attn.reshape(bsz * seq, ATTN_WIDTH)

        h = _merge(h, attn, zg, zg_meta, conv_w[l], conv_b[l][None], wa, wc, wo, seq)

        h, = _ffn(h, ffn2_norm_g[l][None], wg2, wu2, wd2)

    return h.reshape(bsz, seq, d)
```

```python
import functools
import math

import jax
import jax.numpy as jnp
import numpy as np
from jax import lax
from jax.experimental import pallas as pl
from jax.experimental.pallas import tpu as pltpu

D_MODEL = 2048
N_META = 16
N_HEADS = 8
N_KV_HEADS = 2
HEAD_DIM = 128
ATTN_WIDTH = N_HEADS * HEAD_DIM
KV_WIDTH = N_KV_HEADS * HEAD_DIM
N_IDX_HEADS = 16
IDX_DIM = 64
IDX_WIDTH = N_IDX_HEADS * IDX_DIM
TOPK_MAX = 256
CONV_WIDTH = D_MODEL // 2
CONV_K = 3
D_FF = 5632
ROPE_THETA = 500000.0
ROT_DIV = 4
EPS = 1e-6

LANES = 128
VMEM_LIMIT = 56 * 1024 * 1024
NEG_BIG = -1e30
Q_SCALE = math.log2(math.e) / math.sqrt(HEAD_DIM)

FFN_TM, FFN_TF = 1024, 512
PROJ_TM = 256
GATE_TM, GATE_TN = 1024, 1792
MERGE_TM = 256
QB = 256
KC = 256
MW = 128
BISECT_FIXED = 17
BISECT_STEPS = 3
HALO = 16


def _params(sem):
    return pltpu.CompilerParams(dimension_semantics=sem, vmem_limit_bytes=VMEM_LIMIT)


def _rms(x, g):
    ms = jnp.mean(x * x, axis=-1, keepdims=True)
    return x * lax.rsqrt(ms + EPS) * g


def _mm(a, b):
    return jnp.dot(a, b, preferred_element_type=jnp.float32)


def _ffn_kernel(emit_weights, h_ref, g_ref, wg_ref, wu_ref, wd_ref, o_ref, *rest):
    u_scr = rest[-1]
    j = pl.program_id(1)

    @pl.when(j == 0)
    def _():
        h = h_ref[...]
        u_scr[...] = _rms(h, g_ref[...]).astype(jnp.bfloat16)
        o_ref[...] = h

    bf = jnp.bfloat16
    wg, wu, wd = wg_ref[...].astype(bf), wu_ref[...].astype(bf), wd_ref[...].astype(bf)
    if emit_weights:
        for dst, w in zip(rest[:3], (wg, wu, wd)):
            dst[...] = w
    u = u_scr[...]
    gate = _mm(u, wg)
    up = _mm(u, wu)
    a = (gate * jax.nn.sigmoid(gate)) * (up * 0.5)
    o_ref[...] += _mm(a.astype(bf), wd)


def _ffn(h, g, wg, wu, wd, emit_weights=False):
    n = h.shape[0]
    tm = min(FFN_TM, n)
    assert not emit_weights or n == tm
    w_specs = [
        pl.BlockSpec((D_MODEL, FFN_TF), lambda i, j: (0, j)),
        pl.BlockSpec((D_MODEL, FFN_TF), lambda i, j: (0, j)),
        pl.BlockSpec((FFN_TF, D_MODEL), lambda i, j: (j, 0)),
    ]
    out_shape = [jax.ShapeDtypeStruct((n, D_MODEL), jnp.float32)]
    out_specs = [pl.BlockSpec((tm, D_MODEL), lambda i, j: (i, 0))]
    if emit_weights:
        out_shape += [jax.ShapeDtypeStruct(w.shape, jnp.bfloat16) for w in (wg, wu, wd)]
        out_specs += w_specs
    return pl.pallas_call(
        functools.partial(_ffn_kernel, emit_weights),
        out_shape=out_shape,
        grid=(n // tm, D_FF // FFN_TF),
        in_specs=[
            pl.BlockSpec((tm, D_MODEL), lambda i, j: (i, 0)),
            pl.BlockSpec((1, D_MODEL), lambda i, j: (0, 0)),
            *w_specs,
        ],
        out_specs=out_specs,
        scratch_shapes=[pltpu.VMEM((tm, D_MODEL), jnp.bfloat16)],
        compiler_params=_params(("parallel", "arbitrary")),
        name="ffn",
    )(h, g, wg, wu, wd)


def _rope(x, cos, s_lo, s_hi, half):
    n = x.shape[-1]
    return x * cos + pltpu.roll(x, n - half, 1) * s_lo + pltpu.roll(x, half, 1) * s_hi


def _proj_attn_kernel(h_ref, g_ref, wq_ref, wqi_ref, wk_ref, wv_ref, wkw_ref, gq_ref, gk_ref,
                      tab_ref, q_ref, qi_ref, k_ref, v_ref, ki_ref, wi_ref):
    bf = jnp.bfloat16
    u = _rms(h_ref[...], g_ref[...]).astype(bf)
    ca, sa_lo, sa_hi = tab_ref[0], tab_ref[1], tab_ref[2]
    ci, si_lo, si_hi = tab_ref[3], tab_ref[4], tab_ref[5]
    half_a = HEAD_DIM // ROT_DIV // 2
    half_i = IDX_DIM // ROT_DIV // 2
    gq = gq_ref[...]
    gk = gk_ref[...]
    z = _mm(u, wq_ref[...])
    for hd in range(N_HEADS):
        sl = slice(hd * LANES, (hd + 1) * LANES)
        q = _rope(_rms(z[:, sl], gq), ca, sa_lo, sa_hi, half_a)
        q_ref[:, sl] = (q * Q_SCALE).astype(bf)
    z = _mm(u, wqi_ref[...])
    for p in range(IDX_WIDTH // LANES):
        sl = slice(p * LANES, (p + 1) * LANES)
        qi_ref[:, sl] = _rope(z[:, sl], ci, si_lo, si_hi, half_i).astype(bf)
    z = _mm(u, wk_ref[...])
    for hd in range(N_KV_HEADS):
        sl = slice(hd * LANES, (hd + 1) * LANES)
        k_ref[:, sl] = _rope(_rms(z[:, sl], gk), ca, sa_lo, sa_hi, half_a).astype(bf)
    v_ref[...] = _mm(u, wv_ref[...]).astype(bf)
    x = _mm(u, wkw_ref[...])
    lane = lax.broadcasted_iota(jnp.int32, x.shape, 1)
    is_ki = lane < IDX_DIM
    roped = _rope(x, jnp.where(is_ki, ci, 1.0), jnp.where(is_ki, si_lo, 0.0),
                  jnp.where(is_ki, si_hi, 0.0), half_i)
    swapped = pltpu.roll(roped, IDX_DIM, 1)
    ki_ref[...] = jnp.where(is_ki, roped, swapped).astype(bf)
    wi_ref[...] = swapped


def _proj_attn(h, g, wq, wqi, wk, wv, wkw, gq, gk, tabs):
    n = h.shape[0]
    tm = min(PROJ_TM, n)
    nt = tabs.shape[1] // tm
    row = lambda i: (i, 0)
    const = lambda i: (0, 0)
    bf = jnp.bfloat16
    weight = lambda w: pl.BlockSpec(w.shape, const)
    return pl.pallas_call(
        _proj_attn_kernel,
        out_shape=(
            jax.ShapeDtypeStruct((n, ATTN_WIDTH), bf),
            jax.ShapeDtypeStruct((n, IDX_WIDTH), bf),
            jax.ShapeDtypeStruct((n, KV_WIDTH), bf),
            jax.ShapeDtypeStruct((n, KV_WIDTH), bf),
            jax.ShapeDtypeStruct((n, LANES), bf),
            jax.ShapeDtypeStruct((n, LANES), jnp.float32),
        ),
        grid=(n // tm,),
        in_specs=[
            pl.BlockSpec((tm, D_MODEL), row),
            pl.BlockSpec((1, D_MODEL), const),
            weight(wq), weight(wqi), weight(wk), weight(wv), weight(wkw),
            pl.BlockSpec((1, LANES), const),
            pl.BlockSpec((1, LANES), const),
            pl.BlockSpec((6, tm, LANES), lambda i: (0, i % nt, 0)),
        ],
        out_specs=(
            pl.BlockSpec((tm, ATTN_WIDTH), row),
            pl.BlockSpec((tm, IDX_WIDTH), row),
            pl.BlockSpec((tm, KV_WIDTH), row),
            pl.BlockSpec((tm, KV_WIDTH), row),
            pl.BlockSpec((tm, LANES), row),
            pl.BlockSpec((tm, LANES), row),
        ),
        compiler_params=_params(("parallel",)),
        name="proj_attn",
    )(h, g, wq, wqi, wk, wv, wkw, gq, gk, tabs)


def _proj_gate_kernel(h_ref, g_ref, w_ref, o_ref, u_scr):
    @pl.when(pl.program_id(1) == 0)
    def _():
        u_scr[...] = _rms(h_ref[...], g_ref[...]).astype(jnp.bfloat16)

    o_ref[...] = _mm(u_scr[...], w_ref[...]).astype(o_ref.dtype)


def _proj_gate(h, g, w, cols=None):
    n = h.shape[0]
    tm = min(GATE_TM, n)
    cols = w.shape[1] if cols is None else -(-cols // GATE_TN) * GATE_TN
    return pl.pallas_call(
        _proj_gate_kernel,
        out_shape=jax.ShapeDtypeStruct((n, cols), jnp.bfloat16),
        grid=(n // tm, cols // GATE_TN),
        in_specs=[
            pl.BlockSpec((tm, D_MODEL), lambda i, j: (i, 0)),
            pl.BlockSpec((1, D_MODEL), lambda i, j: (0, 0)),
            pl.BlockSpec((D_MODEL, GATE_TN), lambda i, j: (0, j)),
        ],
        out_specs=pl.BlockSpec((tm, GATE_TN), lambda i, j: (i, j)),
        scratch_shapes=[pltpu.VMEM((tm, D_MODEL), jnp.bfloat16)],
        compiler_params=_params(("parallel", "arbitrary")),
        name="proj_gate",
    )(h, g, w)


def _dot_nt(a, b):
    return lax.dot_general(a, b, (((1,), (1,)), ((), ())), preferred_element_type=jnp.float32)


def _attn_kernel(k_sel, n_cast, q_ref, qi_ref, wi_ref, k_ref, v_ref, ki_ref, km_ref, vm_ref,
                 kim_ref, *rest):
    cast_in, o_ref, cast_out = rest[:n_cast], rest[n_cast], rest[n_cast + 1:2 * n_cast + 1]
    (qs_scr, qis_scr, sctm_scr, sct_scr, biasm_scr, bias_scr,
     m_scr, l_scr, acc_scr) = rest[2 * n_cast + 1:]
    for src, dst in zip(cast_in, cast_out):
        dst[...] = src[...].astype(dst.dtype)

    qb = pl.program_id(1)
    n_seq = qb + 1
    rep = N_HEADS // N_KV_HEADS
    hpg = 4
    lane_q = lax.broadcasted_iota(jnp.int32, (QB, LANES), 1)
    neg_inf = float("-inf")

    for g in range(N_KV_HEADS):
        for r in range(rep):
            hd = g * rep + r
            qs_scr[g, r * QB:(r + 1) * QB, :] = q_ref[0, :, hd * LANES:(hd + 1) * LANES]
    for i in range(N_IDX_HEADS // hpg):
        for r in range(hpg):
            hd = i * hpg + r
            pair = qi_ref[0, :, (hd // 2) * LANES:(hd // 2 + 1) * LANES]
            keep = (lane_q < IDX_DIM) if hd % 2 == 0 else (lane_q >= IDX_DIM)
            qis_scr[i, r * QB:(r + 1) * QB, :] = jnp.where(keep, pair, jnp.zeros_like(pair))

    w_all = wi_ref[0]

    def index_scores(ki_c):
        kw = ki_c.shape[0]
        score = jnp.zeros((QB, kw), jnp.float32)
        for i in range(N_IDX_HEADS // hpg):
            d = _dot_nt(qis_scr[i], ki_c)
            for r in range(hpg):
                hd = i * hpg + r
                score = score + jnp.maximum(d[r * QB:(r + 1) * QB], 0.0) * w_all[:, hd:hd + 1]
        return score

    lane_m = lax.broadcasted_iota(jnp.int32, (QB, MW), 1)
    sctm_scr[...] = jnp.where(lane_m < N_META, index_scores(kim_ref[...]), neg_inf).T
    row = lax.broadcasted_iota(jnp.int32, (QB, KC), 0)
    lane = lax.broadcasted_iota(jnp.int32, (QB, KC), 1)

    def score_chunk(c):
        ki_c = ki_ref[0, pl.ds(pl.multiple_of(c * KC, KC), KC), :]
        causal = (c - qb) * KC + lane <= row
        s = jnp.where(causal, index_scores(ki_c), neg_inf)
        bias_scr[c] = s
        sct_scr[c] = s.T

    def score_pair(i, carry):
        score_chunk(2 * i)
        score_chunk(2 * i + 1)
        return carry

    lax.fori_loop(0, n_seq // 2, score_pair, 0)

    @pl.when(n_seq % 2 == 1)
    def _():
        score_chunk(n_seq - 1)
        sct_scr[n_seq] = jnp.full((KC, QB), neg_inf, jnp.float32)

    def over_keys(f, init):
        def pair(i, a):
            c = 2 * i
            a = f(sct_scr[c], MW + c * KC, a)
            return f(sct_scr[c + 1], MW + (c + 1) * KC, a)
        return lax.fori_loop(0, (n_seq + 1) // 2, pair, f(sctm_scr[...], 0, init))

    def fold8(x):
        return jnp.sum(x.reshape(x.shape[0] // 8, 8, QB), axis=0)

    def count(pred):
        part = over_keys(lambda x, p0, a: a + fold8(jnp.where(pred(x, p0), 1.0, 0.0)),
                         jnp.zeros((8, QB), jnp.float32))
        return jnp.sum(part, axis=0, keepdims=True)

    kf = float(k_sel)
    qlane = lax.broadcasted_iota(jnp.int32, (1, QB), 1)
    n_valid = (N_META + 1 + qb * QB + qlane).astype(jnp.float32)

    def minmax(x, p0, carry):
        mn, mx = carry
        x3 = x.reshape(x.shape[0] // 8, 8, QB)
        mx = jnp.maximum(mx, jnp.max(x3, axis=0))
        mn = jnp.minimum(mn, jnp.min(jnp.where(x3 == neg_inf, float("inf"), x3), axis=0))
        return mn, mx

    mn, mx = over_keys(minmax, (jnp.full((8, QB), float("inf"), jnp.float32),
                                jnp.full((8, QB), neg_inf, jnp.float32)))
    mn = jnp.min(mn, axis=0, keepdims=True)
    mx = jnp.max(mx, axis=0, keepdims=True)

    def count_ge(t):
        return count(lambda x, p0: x >= t)

    def midpoint(lo, hi):
        return lo + (hi - lo) * 0.5

    def is_active(lo, hi, cnt_lo):
        mid = midpoint(lo, hi)
        return jnp.where((cnt_lo != kf) & (mid > lo) & (mid < hi), 1.0, 0.0)

    cnt_mx = count_ge(mx)
    at_max = cnt_mx >= kf
    lo0 = jnp.where(at_max, mx, mn)
    cnt0 = jnp.where(at_max, cnt_mx, n_valid)
    act0 = jnp.where(n_valid > kf, is_active(lo0, mx, cnt0), 0.0)

    def bisect_once(state):
        lo, hi, cnt_lo, act = state
        mid = midpoint(lo, hi)
        cnt = count_ge(mid)
        up = (act > 0.5) & (cnt >= kf)
        down = (act > 0.5) & (cnt < kf)
        lo = jnp.where(up, mid, lo)
        cnt_lo = jnp.where(up, cnt, cnt_lo)
        hi = jnp.where(down, mid, hi)
        return lo, hi, cnt_lo, act * is_active(lo, hi, cnt_lo)

    def bisect_trip(st):
        for _ in range(BISECT_STEPS):
            st = bisect_once(st)
        return st

    st = lax.fori_loop(0, BISECT_FIXED, lambda _, st: bisect_once(st), (lo0, mx, cnt0, act0))
    thr, _, cnt_thr, _ = lax.while_loop(lambda st: jnp.max(st[3]) > 0.5, bisect_trip, st)
    has_tie = cnt_thr > kf
    any_tie = jnp.max(jnp.where(has_tie, 1.0, 0.0)) > 0.5

    def write_bias(keep):
        biasm_scr[...] = jnp.where(keep(sctm_scr[...], 0), 0.0, NEG_BIG).T

        def body(c, carry):
            bias_scr[c] = jnp.where(keep(sct_scr[c], MW + c * KC), 0.0, NEG_BIG).T
            return carry
        lax.fori_loop(0, n_seq, body, 0)

    @pl.when(jnp.logical_not(any_tie))
    def _():
        biasm_scr[...] = jnp.where(sctm_scr[...] >= thr, 0.0, NEG_BIG).T
        thr_q = jnp.broadcast_to(thr, (LANES, QB)).T
        thr_q = jnp.concatenate([thr_q] * (KC // LANES), axis=1)

        def body(c, carry):
            bias_scr[c] = jnp.where(bias_scr[c] >= thr_q, 0.0, NEG_BIG)
            return carry
        lax.fori_loop(0, n_seq, body, 0)

    @pl.when(any_tie)
    def _():
        far = 2 ** 30

        def tie_pos(x, p0):
            pos = p0 + lax.broadcasted_iota(jnp.int32, x.shape, 0)
            return jnp.where(x == thr, pos, far)

        need = kf - count(lambda x, p0: x > thr)

        def pos_step(i, last):
            cand = last | lax.shift_left(jnp.int32(1), 11 - i)
            below = count(lambda x, p0: tie_pos(x, p0) < cand)
            return jnp.where(below < need, cand, last)

        last = lax.fori_loop(0, 12, pos_step, jnp.zeros((1, QB), jnp.int32))
        last = jnp.where(has_tie, last, far - 1)
        write_bias(lambda x, p0: jnp.where(x > thr, 1.0,
                                           jnp.where(tie_pos(x, p0) <= last, 1.0, 0.0)) > 0.5)

    m_scr[...] = jnp.full(m_scr.shape, NEG_BIG, jnp.float32)
    l_scr[...] = jnp.zeros(l_scr.shape, jnp.float32)
    acc_scr[...] = jnp.zeros(acc_scr.shape, jnp.float32)

    def attend(bias, k_c, v_c):
        kw = k_c.shape[0]
        for g in range(N_KV_HEADS):
            sl = slice(g * LANES, (g + 1) * LANES)
            s = _dot_nt(qs_scr[g], k_c[:, sl])
            s = jnp.concatenate([s[r * QB:(r + 1) * QB] + bias for r in range(rep)], axis=0)
            m_prev = m_scr[g]
            m_new = jnp.maximum(m_prev, jnp.max(s, axis=-1, keepdims=True))
            alpha = jnp.exp2(m_prev - m_new)
            p = jnp.exp2(s - jnp.concatenate([m_new] * (kw // LANES), axis=1))
            l_scr[g] = alpha * l_scr[g] + jnp.sum(p, axis=-1, keepdims=True)
            acc_scr[g] = alpha * acc_scr[g] + _mm(p.astype(jnp.bfloat16), v_c[:, sl])
            m_scr[g] = m_new

    attend(biasm_scr[...], km_ref[...], vm_ref[...])

    def attend_pair(i, carry):
        c = 2 * i
        rows = pl.ds(pl.multiple_of(c * KC, 2 * KC), 2 * KC)
        attend(jnp.concatenate([bias_scr[c], bias_scr[c + 1]], axis=1),
               k_ref[0, rows, :], v_ref[0, rows, :])
        return carry

    lax.fori_loop(0, n_seq // 2, attend_pair, 0)

    @pl.when(n_seq % 2 == 1)
    def _():
        c = n_seq - 1
        rows = pl.ds(pl.multiple_of(c * KC, KC), KC)
        attend(bias_scr[c], k_ref[0, rows, :], v_ref[0, rows, :])

    for g in range(N_KV_HEADS):
        o = acc_scr[g] / l_scr[g]
        for r in range(rep):
            hd = g * rep + r
            o_ref[0, :, hd * LANES:(hd + 1) * LANES] = o[r * QB:(r + 1) * QB].astype(o_ref.dtype)


def _cast_spec(w, nb, nq):
    rows, cols = w.shape
    if rows % (nb * nq * 16) == 0:
        tr = rows // (nb * nq)
        return pl.BlockSpec((tr, cols), lambda bi, qi_: (bi * nq + qi_, 0))
    assert rows % (nb * 16) == 0 and cols % (nq * LANES) == 0, w.shape
    return pl.BlockSpec((rows // nb, cols // nq), lambda bi, qi_: (bi, qi_))


def _attention(q, qi, wi, k, v, ki, km, vm, kim, k_sel, cast_weights):
    b, s, _ = q.shape
    nq = s // QB
    blk = lambda bi, qi_: (bi, qi_, 0)
    full = lambda bi, qi_: (bi, 0, 0)
    const = lambda bi, qi_: (0, 0)
    rep = N_HEADS // N_KV_HEADS
    f32 = jnp.float32
    cast_specs = [_cast_spec(w, b, nq) for w in cast_weights]
    return pl.pallas_call(
        functools.partial(_attn_kernel, k_sel, len(cast_weights)),
        out_shape=(jax.ShapeDtypeStruct((b, s, ATTN_WIDTH), jnp.bfloat16),
                   *[jax.ShapeDtypeStruct(w.shape, jnp.bfloat16) for w in cast_weights]),
        grid=(b, nq),
        in_specs=[
            pl.BlockSpec((1, QB, ATTN_WIDTH), blk),
            pl.BlockSpec((1, QB, IDX_WIDTH), blk),
            pl.BlockSpec((1, QB, LANES), blk),
            pl.BlockSpec((1, s, KV_WIDTH), full),
            pl.BlockSpec((1, s, KV_WIDTH), full),
            pl.BlockSpec((1, s, LANES), full),
            pl.BlockSpec((MW, KV_WIDTH), const),
            pl.BlockSpec((MW, KV_WIDTH), const),
            pl.BlockSpec((MW, LANES), const),
            *cast_specs,
        ],
        out_specs=(pl.BlockSpec((1, QB, ATTN_WIDTH), blk), *cast_specs),
        scratch_shapes=[
            pltpu.VMEM((N_KV_HEADS, rep * QB, LANES), jnp.bfloat16),
            pltpu.VMEM((N_IDX_HEADS // 4, 4 * QB, LANES), jnp.bfloat16),
            pltpu.VMEM((MW, QB), f32),
            pltpu.VMEM((s // KC, KC, QB), f32),
            pltpu.VMEM((QB, MW), f32),
            pltpu.VMEM((s // KC, QB, KC), f32),
            pltpu.VMEM((N_KV_HEADS, rep * QB, LANES), f32),
            pltpu.VMEM((N_KV_HEADS, rep * QB, LANES), f32),
            pltpu.VMEM((N_KV_HEADS, rep * QB, LANES), f32),
        ],
        compiler_params=_params(("parallel", "arbitrary")),
        name="dsa_attention",
    )(q, qi, wi, k, v, ki, km, vm, kim, *cast_weights)


def _merge_kernel(tiles_per_seq, h_ref, attn_ref, xc_ref, gb_ref, cg_ref, ga0_ref, ga1_ref,
                  gc0_ref, gc1_ref, xc_prev_ref, cg_prev_ref, xc_meta_ref, cg_meta_ref,
                  cw_ref, cb_ref, wa_ref, wc_ref, wo_ref, o_ref):
    f32 = jnp.float32
    first = pl.program_id(0) % tiles_per_seq == 0
    gx = cg_ref[...].astype(f32) * xc_ref[...].astype(f32)
    halo_prev = cg_prev_ref[...].astype(f32) * xc_prev_ref[...].astype(f32)
    halo_meta = cg_meta_ref[...].astype(f32) * xc_meta_ref[...].astype(f32)
    halo = jnp.where(first, halo_meta, halo_prev)
    h1, h2 = halo[HALO - 1:HALO], halo[HALO - 2:HALO - 1]
    row = lax.broadcasted_iota(jnp.int32, gx.shape, 0)
    prev1 = jnp.where(row == 0, h1, pltpu.roll(gx, 1, 0))
    prev2 = jnp.where(row == 0, h2, jnp.where(row == 1, h1, pltpu.roll(gx, 2, 0)))
    cw = cw_ref[...]
    conv = cw[0:1] * prev2 + cw[1:2] * prev1 + cw[2:3] * gx + cb_ref[...]
    feat = (gb_ref[...].astype(f32) * conv).astype(jnp.bfloat16)
    y_conv = _mm(feat, wc_ref[...])
    y_attn = _mm(attn_ref[...], wa_ref[...])
    half = D_MODEL // 2
    sig = lambda r: jax.nn.sigmoid(r[...].astype(f32))
    merged = jnp.concatenate(
        [sig(ga0_ref) * y_attn[:, :half] + sig(gc0_ref) * y_conv[:, :half],
         sig(ga1_ref) * y_attn[:, half:] + sig(gc1_ref) * y_conv[:, half:]], axis=1)
    o_ref[...] = h_ref[...] + _mm(merged.astype(jnp.bfloat16), wo_ref[...])


def _merge(h, attn, zg, zg_meta, conv_w, conv_b, wa, wc, wo, seq):
    n = h.shape[0]
    tm = MERGE_TM
    tiles_per_seq = seq // tm
    row2 = lambda i: (i, 0)
    const = lambda i: (0, 0)
    col = lambda c: pl.BlockSpec((tm, CONV_WIDTH), lambda i: (i, c))
    prev = lambda c: pl.BlockSpec((HALO, CONV_WIDTH),
                                  lambda i: (jnp.maximum(i * (tm // HALO) - 1, 0), c))
    meta = lambda c: pl.BlockSpec((HALO, CONV_WIDTH), lambda i: (N_META // HALO - 1, c))
    single = pl.Buffered(1)
    return pl.pallas_call(
        functools.partial(_merge_kernel, tiles_per_seq),
        out_shape=jax.ShapeDtypeStruct((n, D_MODEL), jnp.float32),
        grid=(n // tm,),
        in_specs=[
            pl.BlockSpec((tm, D_MODEL), row2),
            pl.BlockSpec((tm, ATTN_WIDTH), row2),
            col(0), col(1), col(2), col(3), col(4), col(5), col(6),
            prev(0), prev(2), meta(0), meta(2),
            pl.BlockSpec((CONV_K, CONV_WIDTH), const),
            pl.BlockSpec((1, CONV_WIDTH), const),
            pl.BlockSpec((ATTN_WIDTH, D_MODEL), const, pipeline_mode=single),
            pl.BlockSpec((CONV_WIDTH, D_MODEL), const, pipeline_mode=single),
            pl.BlockSpec((D_MODEL, D_MODEL), const, pipeline_mode=single),
        ],
        out_specs=pl.BlockSpec((tm, D_MODEL), row2),
        compiler_params=_params(("parallel",)),
        name="merge",
    )(h, attn, zg, zg, zg, zg, zg, zg, zg, zg, zg, zg_meta, zg_meta, conv_w, conv_b, wa, wc, wo)


def _rope_tables(pos, head_dim):
    rot = head_dim // ROT_DIV
    half = rot // 2
    inv = ROPE_THETA ** (-jnp.arange(0, rot, 2, dtype=jnp.float32) / rot)
    ang = pos.astype(jnp.float32)[:, None] * inv[None, :]
    cos, sin = jnp.cos(ang), jnp.sin(ang)
    n = pos.shape[0]
    pad = head_dim - rot
    c = jnp.concatenate([cos, cos, jnp.ones((n, pad), jnp.float32)], axis=1)
    zeros_h = jnp.zeros((n, half), jnp.float32)
    zeros_p = jnp.zeros((n, pad), jnp.float32)
    s_lo = jnp.concatenate([-sin, zeros_h, zeros_p], axis=1)
    s_hi = jnp.concatenate([zeros_h, sin, zeros_p], axis=1)
    tile = LANES // head_dim
    return [jnp.tile(t, (1, tile)) for t in (c, s_lo, s_hi)]


def kernel(x, meta_tokens, ffn1_norm_g, ffn1_w_gate, ffn1_w_up, ffn1_w_down, mix_norm_g, w_in, q_norm_g, k_norm_g, conv_w, conv_b, w_attn_branch, w_conv_branch, w_out, ffn2_norm_g, ffn2_w_gate, ffn2_w_up, ffn2_w_down):
    bsz, seq, d = x.shape
    bf = jnp.bfloat16
    k_sel = min(TOPK_MAX, seq // 4)
    assert d == D_MODEL and seq % QB == 0 and k_sel <= KC

    widths = [ATTN_WIDTH, KV_WIDTH, KV_WIDTH, IDX_WIDTH, IDX_DIM, N_IDX_HEADS,
              CONV_WIDTH, CONV_WIDTH, CONV_WIDTH, D_MODEL, D_MODEL]
    offs = [int(o) for o in np.concatenate([[0], np.cumsum(widths)])]
    seg = lambda w, i, j=None: w[:, offs[i]:offs[i + 1 if j is None else j]].astype(bf)

    h = x.reshape(bsz * seq, d)
    hm = meta_tokens.astype(x.dtype)
    pos_seq = jnp.arange(N_META, N_META + seq, dtype=jnp.int32)
    pos_meta = jnp.arange(N_META, dtype=jnp.int32)
    tabs_seq = jnp.stack(_rope_tables(pos_seq, HEAD_DIM) + _rope_tables(pos_seq, IDX_DIM))
    tabs_meta = jnp.stack(_rope_tables(pos_meta, HEAD_DIM) + _rope_tables(pos_meta, IDX_DIM))

    for l in range(ffn1_norm_g.shape[0]):
        wl = w_in[l]
        w_attn = (seg(wl, 0), seg(wl, 3), seg(wl, 1), seg(wl, 2),
                  jnp.pad(seg(wl, 4, 6), ((0, 0), (0, LANES - IDX_DIM - N_IDX_HEADS))))
        w_g = seg(wl, 6, 11)
        g1 = ffn1_norm_g[l][None]
        gm = mix_norm_g[l][None]

        hm, wg1, wu1, wd1 = _ffn(hm, g1, ffn1_w_gate[l], ffn1_w_up[l], ffn1_w_down[l],
                                 emit_weights=True)
        h, = _ffn(h, g1, wg1, wu1, wd1)

        gq, gk = q_norm_g[l][None], k_norm_g[l][None]
        q, qi, k, v, ki, wi = _proj_attn(h, gm, *w_attn, gq, gk, tabs_seq)
        _, _, km, vm, kim, _ = _proj_attn(hm, gm, *w_attn, gq, gk, tabs_meta)
        zg = _proj_gate(h, gm, w_g)
        zg_meta = _proj_gate(hm, gm, w_g, cols=3 * CONV_WIDTH)

        pad_rows = lambda a: jnp.pad(a, ((0, MW - N_META), (0, 0)))
        r3 = lambda a: a.reshape(bsz, seq, a.shape[-1])
        later = [w_attn_branch[l], w_conv_branch[l], w_out[l],
                 ffn2_w_gate[l], ffn2_w_up[l], ffn2_w_down[l]]
        attn, wa, wc, wo, wg2, wu2, wd2 = _attention(
            r3(q), r3(qi), r3(wi), r3(k), r3(v), r3(ki),
            pad_rows(km), pad_rows(vm), pad_rows(kim), k_sel, later)
        attn = attn.reshape(bsz * seq, ATTN_WIDTH)

        h = _merge(h, attn, zg, zg_meta, conv_w[l], conv_b[l][None], wa, wc, wo, seq)

        h, = _ffn(h, ffn2_norm_g[l][None], wg2, wu2, wd2)

    return h.reshape(bsz, seq, d)
```

```python
import functools
import math

import jax
import jax.numpy as jnp
import numpy as np
from jax import lax
from jax.experimental import pallas as pl
from jax.experimental.pallas import tpu as pltpu

D_MODEL = 2048
N_META = 16
N_HEADS = 8
N_KV_HEADS = 2
HEAD_DIM = 128
ATTN_WIDTH = N_HEADS * HEAD_DIM
KV_WIDTH = N_KV_HEADS * HEAD_DIM
N_IDX_HEADS = 16
IDX_DIM = 64
IDX_WIDTH = N_IDX_HEADS * IDX_DIM
TOPK_MAX = 256
CONV_WIDTH = D_MODEL // 2
CONV_K = 3
D_FF = 5632
ROPE_THETA = 500000.0
ROT_DIV = 4
EPS = 1e-6

LANES = 128
VMEM_LIMIT = 56 * 1024 * 1024
NEG_BIG = -1e30
Q_SCALE = math.log2(math.e) / math.sqrt(HEAD_DIM)

FFN_TM, FFN_TF = 1024, 512
PROJ_TM = 256
GATE_TM, GATE_TN = 1024, 1792
MERGE_TM = 256
QB = 256
KC = 256
MW = 128
IDX_HPG = 1
BISECT_FIXED = 17
BISECT_STEPS = 3
HALO = 16


def _params(sem):
    return pltpu.CompilerParams(dimension_semantics=sem, vmem_limit_bytes=VMEM_LIMIT)


def _rms(x, g):
    ms = jnp.mean(x * x, axis=-1, keepdims=True)
    return x * lax.rsqrt(ms + EPS) * g


def _mm(a, b):
    return jnp.dot(a, b, preferred_element_type=jnp.float32)


def _ffn_kernel(emit_weights, h_ref, g_ref, wg_ref, wu_ref, wd_ref, o_ref, *rest):
    u_scr = rest[-1]
    j = pl.program_id(1)

    @pl.when(j == 0)
    def _():
        h = h_ref[...]
        u_scr[...] = _rms(h, g_ref[...]).astype(jnp.bfloat16)
        o_ref[...] = h

    bf = jnp.bfloat16
    wg, wu, wd = wg_ref[...].astype(bf), wu_ref[...].astype(bf), wd_ref[...].astype(bf)
    if emit_weights:
        for dst, w in zip(rest[:3], (wg, wu, wd)):
            dst[...] = w
    u = u_scr[...]
    gate = _mm(u, wg)
    up = _mm(u, wu)
    a = (gate * jax.nn.sigmoid(gate)) * (up * 0.5)
    o_ref[...] += _mm(a.astype(bf), wd)


def _ffn(h, g, wg, wu, wd, emit_weights=False):
    n = h.shape[0]
    tm = min(FFN_TM, n)
    assert not emit_weights or n == tm
    w_specs = [
        pl.BlockSpec((D_MODEL, FFN_TF), lambda i, j: (0, j)),
        pl.BlockSpec((D_MODEL, FFN_TF), lambda i, j: (0, j)),
        pl.BlockSpec((FFN_TF, D_MODEL), lambda i, j: (j, 0)),
    ]
    out_shape = [jax.ShapeDtypeStruct((n, D_MODEL), jnp.float32)]
    out_specs = [pl.BlockSpec((tm, D_MODEL), lambda i, j: (i, 0))]
    if emit_weights:
        out_shape += [jax.ShapeDtypeStruct(w.shape, jnp.bfloat16) for w in (wg, wu, wd)]
        out_specs += w_specs
    return pl.pallas_call(
        functools.partial(_ffn_kernel, emit_weights),
        out_shape=out_shape,
        grid=(n // tm, D_FF // FFN_TF),
        in_specs=[
            pl.BlockSpec((tm, D_MODEL), lambda i, j: (i, 0)),
            pl.BlockSpec((1, D_MODEL), lambda i, j: (0, 0)),
            *w_specs,
        ],
        out_specs=out_specs,
        scratch_shapes=[pltpu.VMEM((tm, D_MODEL), jnp.bfloat16)],
        compiler_params=_params(("parallel", "arbitrary")),
        name="ffn",
    )(h, g, wg, wu, wd)


def _rope(x, cos, s_lo, s_hi, half):
    n = x.shape[-1]
    return x * cos + pltpu.roll(x, n - half, 1) * s_lo + pltpu.roll(x, half, 1) * s_hi


def _proj_attn_kernel(h_ref, g_ref, wq_ref, wqi_ref, wk_ref, wv_ref, wkw_ref, gq_ref, gk_ref,
                      tab_ref, q_ref, qi_ref, k_ref, v_ref, ki_ref, wi_ref):
    bf = jnp.bfloat16
    u = _rms(h_ref[...], g_ref[...]).astype(bf)
    ca, sa_lo, sa_hi = tab_ref[0], tab_ref[1], tab_ref[2]
    ci, si_lo, si_hi = tab_ref[3], tab_ref[4], tab_ref[5]
    half_a = HEAD_DIM // ROT_DIV // 2
    half_i = IDX_DIM // ROT_DIV // 2
    gq = gq_ref[...]
    gk = gk_ref[...]
    z = _dot_nt(u, wq_ref[...])
    for hd in range(N_HEADS):
        sl = slice(hd * LANES, (hd + 1) * LANES)
        q = _rope(_rms(z[:, sl], gq), ca, sa_lo, sa_hi, half_a)
        q_ref[:, sl] = (q * Q_SCALE).astype(bf)
    z = _dot_nt(u, wqi_ref[...])
    for p in range(IDX_WIDTH // LANES):
        sl = slice(p * LANES, (p + 1) * LANES)
        qi_ref[:, sl] = _rope(z[:, sl], ci, si_lo, si_hi, half_i).astype(bf)
    z = _dot_nt(u, wk_ref[...])
    for hd in range(N_KV_HEADS):
        sl = slice(hd * LANES, (hd + 1) * LANES)
        k_ref[:, sl] = _rope(_rms(z[:, sl], gk), ca, sa_lo, sa_hi, half_a).astype(bf)
    v_ref[...] = _dot_nt(u, wv_ref[...]).astype(bf)
    x = _dot_nt(u, wkw_ref[...])
    lane = lax.broadcasted_iota(jnp.int32, x.shape, 1)
    is_ki = lane < IDX_DIM
    roped = _rope(x, jnp.where(is_ki, ci, 1.0), jnp.where(is_ki, si_lo, 0.0),
                  jnp.where(is_ki, si_hi, 0.0), half_i)
    swapped = pltpu.roll(roped, IDX_DIM, 1)
    ki_ref[...] = jnp.where(is_ki, roped, swapped).astype(bf)
    wi_ref[...] = swapped


def _proj_attn(h, g, wq, wqi, wk, wv, wkw, gq, gk, tabs):
    n = h.shape[0]
    tm = min(PROJ_TM, n)
    nt = tabs.shape[1] // tm
    row = lambda i: (i, 0)
    const = lambda i: (0, 0)
    bf = jnp.bfloat16
    weight = lambda w: pl.BlockSpec(w.shape, const)
    return pl.pallas_call(
        _proj_attn_kernel,
        out_shape=(
            jax.ShapeDtypeStruct((n, ATTN_WIDTH), bf),
            jax.ShapeDtypeStruct((n, IDX_WIDTH), bf),
            jax.ShapeDtypeStruct((n, KV_WIDTH), bf),
            jax.ShapeDtypeStruct((n, KV_WIDTH), bf),
            jax.ShapeDtypeStruct((n, LANES), bf),
            jax.ShapeDtypeStruct((n, LANES), jnp.float32),
        ),
        grid=(n // tm,),
        in_specs=[
            pl.BlockSpec((tm, D_MODEL), row),
            pl.BlockSpec((1, D_MODEL), const),
            weight(wq), weight(wqi), weight(wk), weight(wv), weight(wkw),
            pl.BlockSpec((1, LANES), const),
            pl.BlockSpec((1, LANES), const),
            pl.BlockSpec((6, tm, LANES), lambda i: (0, i % nt, 0)),
        ],
        out_specs=(
            pl.BlockSpec((tm, ATTN_WIDTH), row),
            pl.BlockSpec((tm, IDX_WIDTH), row),
            pl.BlockSpec((tm, KV_WIDTH), row),
            pl.BlockSpec((tm, KV_WIDTH), row),
            pl.BlockSpec((tm, LANES), row),
            pl.BlockSpec((tm, LANES), row),
        ),
        compiler_params=_params(("parallel",)),
        name="proj_attn",
    )(h, g, wq, wqi, wk, wv, wkw, gq, gk, tabs)


def _proj_gate_kernel(h_ref, g_ref, w_ref, o_ref, u_scr):
    @pl.when(pl.program_id(1) == 0)
    def _():
        u_scr[...] = _rms(h_ref[...], g_ref[...]).astype(jnp.bfloat16)

    o_ref[...] = _dot_nt(u_scr[...], w_ref[...]).astype(o_ref.dtype)


def _proj_gate(h, g, wt, row0, cols):
    n = h.shape[0]
    tm = min(GATE_TM, n)
    cols = -(-cols // GATE_TN) * GATE_TN
    assert row0 + cols <= wt.shape[0] and row0 % 16 == 0
    return pl.pallas_call(
        _proj_gate_kernel,
        out_shape=jax.ShapeDtypeStruct((n, cols), jnp.bfloat16),
        grid=(n // tm, cols // GATE_TN),
        in_specs=[
            pl.BlockSpec((tm, D_MODEL), lambda i, j: (i, 0)),
            pl.BlockSpec((1, D_MODEL), lambda i, j: (0, 0)),
            pl.BlockSpec((pl.Element(GATE_TN), pl.Element(D_MODEL)),
                         lambda i, j: (pl.multiple_of(row0 + j * GATE_TN, 16), 0)),
        ],
        out_specs=pl.BlockSpec((tm, GATE_TN), lambda i, j: (i, j)),
        scratch_shapes=[pltpu.VMEM((tm, D_MODEL), jnp.bfloat16)],
        compiler_params=_params(("parallel", "arbitrary")),
        name="proj_gate",
    )(h, g, wt)


def _dot_nt(a, b):
    return lax.dot_general(a, b, (((1,), (1,)), ((), ())), preferred_element_type=jnp.float32)


def _attn_kernel(k_sel, n_cast, q_ref, qi_ref, wi_ref, k_ref, v_ref, ki_ref, km_ref, vm_ref,
                 kim_ref, *rest):
    cast_in, o_ref, cast_out = rest[:n_cast], rest[n_cast], rest[n_cast + 1:2 * n_cast + 1]
    (qs_scr, qis_scr, sctm_scr, sct_scr, biasm_scr, bias_scr,
     m_scr, l_scr, acc_scr) = rest[2 * n_cast + 1:]
    for src, dst in zip(cast_in, cast_out):
        dst[...] = src[...].astype(dst.dtype)

    qb = pl.program_id(1)
    n_seq = qb + 1
    rep = N_HEADS // N_KV_HEADS
    hpg = IDX_HPG
    lane_q = lax.broadcasted_iota(jnp.int32, (QB, LANES), 1)
    neg_inf = float("-inf")

    for g in range(N_KV_HEADS):
        for r in range(rep):
            hd = g * rep + r
            qs_scr[g, r * QB:(r + 1) * QB, :] = q_ref[0, :, hd * LANES:(hd + 1) * LANES]
    for i in range(N_IDX_HEADS // hpg):
        for r in range(hpg):
            hd = i * hpg + r
            pair = qi_ref[0, :, (hd // 2) * LANES:(hd // 2 + 1) * LANES]
            keep = (lane_q < IDX_DIM) if hd % 2 == 0 else (lane_q >= IDX_DIM)
            qis_scr[i, r * QB:(r + 1) * QB, :] = jnp.where(keep, pair, jnp.zeros_like(pair))

    w_all = wi_ref[0]

    def index_scores(ki_c):
        kw = ki_c.shape[0]
        score = jnp.zeros((QB, kw), jnp.float32)
        for i in range(N_IDX_HEADS // hpg):
            d = _dot_nt(qis_scr[i], ki_c)
            for r in range(hpg):
                hd = i * hpg + r
                score = score + jnp.maximum(d[r * QB:(r + 1) * QB], 0.0) * w_all[:, hd:hd + 1]
        return score

    lane_m = lax.broadcasted_iota(jnp.int32, (QB, MW), 1)
    sctm_scr[...] = jnp.where(lane_m < N_META, index_scores(kim_ref[...]), neg_inf).T
    row = lax.broadcasted_iota(jnp.int32, (QB, KC), 0)
    lane = lax.broadcasted_iota(jnp.int32, (QB, KC), 1)

    def score_chunk(c):
        ki_c = ki_ref[0, pl.ds(pl.multiple_of(c * KC, KC), KC), :]
        causal = (c - qb) * KC + lane <= row
        s = jnp.where(causal, index_scores(ki_c), neg_inf)
        bias_scr[c] = s
        sct_scr[c] = s.T

    def score_pair(i, carry):
        score_chunk(2 * i)
        score_chunk(2 * i + 1)
        return carry

    lax.fori_loop(0, n_seq // 2, score_pair, 0)

    @pl.when(n_seq % 2 == 1)
    def _():
        score_chunk(n_seq - 1)
        sct_scr[n_seq] = jnp.full((KC, QB), neg_inf, jnp.float32)

    def over_keys(f, init):
        def pair(i, a):
            c = 2 * i
            a = f(sct_scr[c], MW + c * KC, a)
            return f(sct_scr[c + 1], MW + (c + 1) * KC, a)
        return lax.fori_loop(0, (n_seq + 1) // 2, pair, f(sctm_scr[...], 0, init))

    def fold8(x):
        return jnp.sum(x.reshape(x.shape[0] // 8, 8, QB), axis=0)

    def count(pred):
        part = over_keys(lambda x, p0, a: a + fold8(jnp.where(pred(x, p0), 1.0, 0.0)),
                         jnp.zeros((8, QB), jnp.float32))
        return jnp.sum(part, axis=0, keepdims=True)

    kf = float(k_sel)
    qlane = lax.broadcasted_iota(jnp.int32, (1, QB), 1)
    n_valid = (N_META + 1 + qb * QB + qlane).astype(jnp.float32)

    def minmax(x, p0, carry):
        mn, mx = carry
        x3 = x.reshape(x.shape[0] // 8, 8, QB)
        mx = jnp.maximum(mx, jnp.max(x3, axis=0))
        mn = jnp.minimum(mn, jnp.min(jnp.where(x3 == neg_inf, float("inf"), x3), axis=0))
        return mn, mx

    mn, mx = over_keys(minmax, (jnp.full((8, QB), float("inf"), jnp.float32),
                                jnp.full((8, QB), neg_inf, jnp.float32)))
    mn = jnp.min(mn, axis=0, keepdims=True)
    mx = jnp.max(mx, axis=0, keepdims=True)

    def count_ge(t):
        return count(lambda x, p0: x >= t)

    def midpoint(lo, hi):
        return lo + (hi - lo) * 0.5

    def is_active(lo, hi, cnt_lo):
        mid = midpoint(lo, hi)
        return jnp.where((cnt_lo != kf) & (mid > lo) & (mid < hi), 1.0, 0.0)

    cnt_mx = count_ge(mx)
    at_max = cnt_mx >= kf
    lo0 = jnp.where(at_max, mx, mn)
    cnt0 = jnp.where(at_max, cnt_mx, n_valid)
    act0 = jnp.where(n_valid > kf, is_active(lo0, mx, cnt0), 0.0)

    def bisect_once(state):
        lo, hi, cnt_lo, act = state
        mid = midpoint(lo, hi)
        cnt = count_ge(mid)
        up = (act > 0.5) & (cnt >= kf)
        down = (act > 0.5) & (cnt < kf)
        lo = jnp.where(up, mid, lo)
        cnt_lo = jnp.where(up, cnt, cnt_lo)
        hi = jnp.where(down, mid, hi)
        return lo, hi, cnt_lo, act * is_active(lo, hi, cnt_lo)

    def bisect_trip(st):
        for _ in range(BISECT_STEPS):
            st = bisect_once(st)
        return st

    st = lax.fori_loop(0, BISECT_FIXED, lambda _, st: bisect_once(st), (lo0, mx, cnt0, act0))
    thr, _, cnt_thr, _ = lax.while_loop(lambda st: jnp.max(st[3]) > 0.5, bisect_trip, st)
    has_tie = cnt_thr > kf
    any_tie = jnp.max(jnp.where(has_tie, 1.0, 0.0)) > 0.5

    def write_bias(keep):
        biasm_scr[...] = jnp.where(keep(sctm_scr[...], 0), 0.0, NEG_BIG).T

        def body(c, carry):
            bias_scr[c] = jnp.where(keep(sct_scr[c], MW + c * KC), 0.0, NEG_BIG).T
            return carry
        lax.fori_loop(0, n_seq, body, 0)

    @pl.when(jnp.logical_not(any_tie))
    def _():
        biasm_scr[...] = jnp.where(sctm_scr[...] >= thr, 0.0, NEG_BIG).T
        thr_q = jnp.broadcast_to(thr, (LANES, QB)).T
        thr_q = jnp.concatenate([thr_q] * (KC // LANES), axis=1)

        def body(c, carry):
            bias_scr[c] = jnp.where(bias_scr[c] >= thr_q, 0.0, NEG_BIG)
            return carry
        lax.fori_loop(0, n_seq, body, 0)

    @pl.when(any_tie)
    def _():
        pos_bits = (MW + k_ref.shape[1] - 1).bit_length()
        far = 1 << pos_bits

        def tie_pos(x, p0):
            pos = p0 + lax.broadcasted_iota(jnp.int32, x.shape, 0)
            return jnp.where(x == thr, pos, far)

        need = kf - count(lambda x, p0: x > thr)

        def pos_step(i, last):
            cand = last | lax.shift_left(jnp.int32(1), pos_bits - 1 - i)
            below = count(lambda x, p0: tie_pos(x, p0) < cand)
            return jnp.where(below < need, cand, last)

        last = lax.fori_loop(0, pos_bits, pos_step, jnp.zeros((1, QB), jnp.int32))
        last = jnp.where(has_tie, last, far - 1)
        write_bias(lambda x, p0: jnp.where(x > thr, 1.0,
                                           jnp.where(tie_pos(x, p0) <= last, 1.0, 0.0)) > 0.5)

    m_scr[...] = jnp.full(m_scr.shape, NEG_BIG, jnp.float32)
    l_scr[...] = jnp.zeros(l_scr.shape, jnp.float32)
    acc_scr[...] = jnp.zeros(acc_scr.shape, jnp.float32)

    def attend(bias, k_c, v_c):
        kw = k_c.shape[0]
        for g in range(N_KV_HEADS):
            sl = slice(g * LANES, (g + 1) * LANES)
            s = _dot_nt(qs_scr[g], k_c[:, sl])
            s = jnp.concatenate([s[r * QB:(r + 1) * QB] + bias for r in range(rep)], axis=0)
            m_prev = m_scr[g]
            m_new = jnp.maximum(m_prev, jnp.max(s, axis=-1, keepdims=True))
            alpha = jnp.exp2(m_prev - m_new)
            p = jnp.exp2(s - jnp.concatenate([m_new] * (kw // LANES), axis=1))
            l_scr[g] = alpha * l_scr[g] + jnp.sum(p, axis=-1, keepdims=True)
            acc_scr[g] = alpha * acc_scr[g] + _mm(p.astype(jnp.bfloat16), v_c[:, sl])
            m_scr[g] = m_new

    attend(biasm_scr[...], km_ref[...], vm_ref[...])

    def attend_pair(i, carry):
        c = 2 * i
        rows = pl.ds(pl.multiple_of(c * KC, 2 * KC), 2 * KC)
        attend(jnp.concatenate([bias_scr[c], bias_scr[c + 1]], axis=1),
               k_ref[0, rows, :], v_ref[0, rows, :])
        return carry

    lax.fori_loop(0, n_seq // 2, attend_pair, 0)

    @pl.when(n_seq % 2 == 1)
    def _():
        c = n_seq - 1
        rows = pl.ds(pl.multiple_of(c * KC, KC), KC)
        attend(bias_scr[c], k_ref[0, rows, :], v_ref[0, rows, :])

    for g in range(N_KV_HEADS):
        o = acc_scr[g] / l_scr[g]
        for r in range(rep):
            hd = g * rep + r
            o_ref[0, :, hd * LANES:(hd + 1) * LANES] = o[r * QB:(r + 1) * QB].astype(o_ref.dtype)


def _cast_spec(w, nb, nq):
    rows, cols = w.shape
    if rows % (nb * nq * 16) == 0:
        tr = rows // (nb * nq)
        return pl.BlockSpec((tr, cols), lambda bi, qi_: (bi * nq + qi_, 0))
    assert rows % (nb * 16) == 0 and cols % (nq * LANES) == 0, w.shape
    return pl.BlockSpec((rows // nb, cols // nq), lambda bi, qi_: (bi, qi_))


def _attention(q, qi, wi, k, v, ki, km, vm, kim, k_sel, cast_weights):
    b, s, _ = q.shape
    nq = s // QB
    blk = lambda bi, qi_: (bi, qi_, 0)
    full = lambda bi, qi_: (bi, 0, 0)
    const = lambda bi, qi_: (0, 0)
    rep = N_HEADS // N_KV_HEADS
    f32 = jnp.float32
    cast_specs = [_cast_spec(w, b, nq) for w in cast_weights]
    return pl.pallas_call(
        functools.partial(_attn_kernel, k_sel, len(cast_weights)),
        out_shape=(jax.ShapeDtypeStruct((b, s, ATTN_WIDTH), jnp.bfloat16),
                   *[jax.ShapeDtypeStruct(w.shape, jnp.bfloat16) for w in cast_weights]),
        grid=(b, nq),
        in_specs=[
            pl.BlockSpec((1, QB, ATTN_WIDTH), blk),
            pl.BlockSpec((1, QB, IDX_WIDTH), blk),
            pl.BlockSpec((1, QB, LANES), blk),
            pl.BlockSpec((1, s, KV_WIDTH), full),
            pl.BlockSpec((1, s, KV_WIDTH), full),
            pl.BlockSpec((1, s, LANES), full),
            pl.BlockSpec((MW, KV_WIDTH), const),
            pl.BlockSpec((MW, KV_WIDTH), const),
            pl.BlockSpec((MW, LANES), const),
            *cast_specs,
        ],
        out_specs=(pl.BlockSpec((1, QB, ATTN_WIDTH), blk), *cast_specs),
        scratch_shapes=[
            pltpu.VMEM((N_KV_HEADS, rep * QB, LANES), jnp.bfloat16),
            pltpu.VMEM((N_IDX_HEADS // IDX_HPG, IDX_HPG * QB, LANES), jnp.bfloat16),
            pltpu.VMEM((MW, QB), f32),
            pltpu.VMEM((s // KC, KC, QB), f32),
            pltpu.VMEM((QB, MW), f32),
            pltpu.VMEM((s // KC, QB, KC), f32),
            pltpu.VMEM((N_KV_HEADS, rep * QB, LANES), f32),
            pltpu.VMEM((N_KV_HEADS, rep * QB, LANES), f32),
            pltpu.VMEM((N_KV_HEADS, rep * QB, LANES), f32),
        ],
        compiler_params=_params(("parallel", "arbitrary")),
        name="dsa_attention",
    )(q, qi, wi, k, v, ki, km, vm, kim, *cast_weights)


def _merge_kernel(tiles_per_seq, h_ref, attn_ref, xc_ref, gb_ref, cg_ref, ga0_ref, ga1_ref,
                  gc0_ref, gc1_ref, xc_prev_ref, cg_prev_ref, xc_meta_ref, cg_meta_ref,
                  cw_ref, cb_ref, wa_ref, wc_ref, wo_ref, o_ref):
    f32 = jnp.float32
    first = pl.program_id(0) % tiles_per_seq == 0
    gx = cg_ref[...].astype(f32) * xc_ref[...].astype(f32)
    halo_prev = cg_prev_ref[...].astype(f32) * xc_prev_ref[...].astype(f32)
    halo_meta = cg_meta_ref[...].astype(f32) * xc_meta_ref[...].astype(f32)
    halo = jnp.where(first, halo_meta, halo_prev)
    h1, h2 = halo[HALO - 1:HALO], halo[HALO - 2:HALO - 1]
    row = lax.broadcasted_iota(jnp.int32, gx.shape, 0)
    prev1 = jnp.where(row == 0, h1, pltpu.roll(gx, 1, 0))
    prev2 = jnp.where(row == 0, h2, jnp.where(row == 1, h1, pltpu.roll(gx, 2, 0)))
    cw = cw_ref[...]
    conv = cw[0:1] * prev2 + cw[1:2] * prev1 + cw[2:3] * gx + cb_ref[...]
    feat = (gb_ref[...].astype(f32) * conv).astype(jnp.bfloat16)
    y_conv = _mm(feat, wc_ref[...])
    y_attn = _mm(attn_ref[...], wa_ref[...])
    half = D_MODEL // 2
    sig = lambda r: jax.nn.sigmoid(r[...].astype(f32))
    merged = jnp.concatenate(
        [sig(ga0_ref) * y_attn[:, :half] + sig(gc0_ref) * y_conv[:, :half],
         sig(ga1_ref) * y_attn[:, half:] + sig(gc1_ref) * y_conv[:, half:]], axis=1)
    o_ref[...] = h_ref[...] + _mm(merged.astype(jnp.bfloat16), wo_ref[...])


def _merge(h, attn, zg, zg_meta, conv_w, conv_b, wa, wc, wo, seq):
    n = h.shape[0]
    tm = MERGE_TM
    tiles_per_seq = seq // tm
    row2 = lambda i: (i, 0)
    const = lambda i: (0, 0)
    col = lambda c: pl.BlockSpec((tm, CONV_WIDTH), lambda i: (i, c))
    prev = lambda c: pl.BlockSpec((HALO, CONV_WIDTH),
                                  lambda i: (jnp.maximum(i * (tm // HALO) - 1, 0), c))
    meta = lambda c: pl.BlockSpec((HALO, CONV_WIDTH), lambda i: (N_META // HALO - 1, c))
    single = pl.Buffered(1)
    return pl.pallas_call(
        functools.partial(_merge_kernel, tiles_per_seq),
        out_shape=jax.ShapeDtypeStruct((n, D_MODEL), jnp.float32),
        grid=(n // tm,),
        in_specs=[
            pl.BlockSpec((tm, D_MODEL), row2),
            pl.BlockSpec((tm, ATTN_WIDTH), row2),
            col(0), col(1), col(2), col(3), col(4), col(5), col(6),
            prev(0), prev(2), meta(0), meta(2),
            pl.BlockSpec((CONV_K, CONV_WIDTH), const),
            pl.BlockSpec((1, CONV_WIDTH), const),
            pl.BlockSpec((ATTN_WIDTH, D_MODEL), const, pipeline_mode=single),
            pl.BlockSpec((CONV_WIDTH, D_MODEL), const, pipeline_mode=single),
            pl.BlockSpec((D_MODEL, D_MODEL), const, pipeline_mode=single),
        ],
        out_specs=pl.BlockSpec((tm, D_MODEL), row2),
        compiler_params=_params(("parallel",)),
        name="merge",
    )(h, attn, zg, zg, zg, zg, zg, zg, zg, zg, zg, zg_meta, zg_meta, conv_w, conv_b, wa, wc, wo)


def _rope_tables(pos, head_dim):
    rot = head_dim // ROT_DIV
    half = rot // 2
    inv = ROPE_THETA ** (-jnp.arange(0, rot, 2, dtype=jnp.float32) / rot)
    ang = pos.astype(jnp.float32)[:, None] * inv[None, :]
    cos, sin = jnp.cos(ang), jnp.sin(ang)
    n = pos.shape[0]
    pad = head_dim - rot
    c = jnp.concatenate([cos, cos, jnp.ones((n, pad), jnp.float32)], axis=1)
    zeros_h = jnp.zeros((n, half), jnp.float32)
    zeros_p = jnp.zeros((n, pad), jnp.float32)
    s_lo = jnp.concatenate([-sin, zeros_h, zeros_p], axis=1)
    s_hi = jnp.concatenate([zeros_h, sin, zeros_p], axis=1)
    tile = LANES // head_dim
    return [jnp.tile(t, (1, tile)) for t in (c, s_lo, s_hi)]


def kernel(x, meta_tokens, ffn1_norm_g, ffn1_w_gate, ffn1_w_up, ffn1_w_down, mix_norm_g, w_in, q_norm_g, k_norm_g, conv_w, conv_b, w_attn_branch, w_conv_branch, w_out, ffn2_norm_g, ffn2_w_gate, ffn2_w_up, ffn2_w_down):
    bsz, seq, d = x.shape
    bf = jnp.bfloat16
    k_sel = min(TOPK_MAX, seq // 4)
    assert d == D_MODEL and seq % QB == 0 and k_sel <= KC

    widths = [ATTN_WIDTH, KV_WIDTH, KV_WIDTH, IDX_WIDTH, IDX_DIM, N_IDX_HEADS,
              CONV_WIDTH, CONV_WIDTH, CONV_WIDTH, D_MODEL, D_MODEL]
    offs = [int(o) for o in np.concatenate([[0], np.cumsum(widths)])]
    seg = lambda wt, i, j=None: wt[offs[i]:offs[i + 1 if j is None else j], :]

    h = x.reshape(bsz * seq, d)
    hm = meta_tokens.astype(x.dtype)
    pos_seq = jnp.arange(N_META, N_META + seq, dtype=jnp.int32)
    pos_meta = jnp.arange(N_META, dtype=jnp.int32)
    tabs_seq = jnp.stack(_rope_tables(pos_seq, HEAD_DIM) + _rope_tables(pos_seq, IDX_DIM))
    tabs_meta = jnp.stack(_rope_tables(pos_meta, HEAD_DIM) + _rope_tables(pos_meta, IDX_DIM))

    for l in range(ffn1_norm_g.shape[0]):
        wl = w_in[l].T.astype(bf)
        w_attn = (seg(wl, 0), seg(wl, 3), seg(wl, 1), seg(wl, 2),
                  jnp.pad(seg(wl, 4, 6), ((0, LANES - IDX_DIM - N_IDX_HEADS), (0, 0))))
        g1 = ffn1_norm_g[l][None]
        gm = mix_norm_g[l][None]

        hm, wg1, wu1, wd1 = _ffn(hm, g1, ffn1_w_gate[l], ffn1_w_up[l], ffn1_w_down[l],
                                 emit_weights=True)
        h, = _ffn(h, g1, wg1, wu1, wd1)

        gq, gk = q_norm_g[l][None], k_norm_g[l][None]
        q, qi, k, v, ki, wi = _proj_attn(h, gm, *w_attn, gq, gk, tabs_seq)
        _, _, km, vm, kim, _ = _proj_attn(hm, gm, *w_attn, gq, gk, tabs_meta)
        zg = _proj_gate(h, gm, wl, offs[6], offs[11] - offs[6])
        zg_meta = _proj_gate(hm, gm, wl, offs[6], 3 * CONV_WIDTH)

        pad_rows = lambda a: jnp.pad(a, ((0, MW - N_META), (0, 0)))
        r3 = lambda a: a.reshape(bsz, seq, a.shape[-1])
        later = [w_attn_branch[l], w_conv_branch[l], w_out[l],
                 ffn2_w_gate[l], ffn2_w_up[l], ffn2_w_down[l]]
        attn, wa, wc, wo, wg2, wu2, wd2 = _attention(
            r3(q), r3(qi), r3(wi), r3(k), r3(v), r3(ki),
            pad_rows(km), pad_rows(vm), pad_rows(kim), k_sel, later)
        attn = attn.reshape(bsz * seq, ATTN_WIDTH)

        h = _merge(h, attn, zg, zg_meta, conv_w[l], conv_b[l][None], wa, wc, wo, seq)

        h, = _ffn(h, ffn2_norm_g[l][None], wg2, wu2, wd2)

    return h.reshape(bsz, seq, d)
```

```python
import functools
import math

import jax
import jax.numpy as jnp
import numpy as np
from jax import lax
from jax.experimental import pallas as pl
from jax.experimental.pallas import tpu as pltpu

D_MODEL = 2048
N_META = 16
N_HEADS = 8
N_KV_HEADS = 2
HEAD_DIM = 128
ATTN_WIDTH = N_HEADS * HEAD_DIM
KV_WIDTH = N_KV_HEADS * HEAD_DIM
N_IDX_HEADS = 16
IDX_DIM = 64
IDX_WIDTH = N_IDX_HEADS * IDX_DIM
TOPK_MAX = 256
CONV_WIDTH = D_MODEL // 2
CONV_K = 3
D_FF = 5632
ROPE_THETA = 500000.0
ROT_DIV = 4
EPS = 1e-6

LANES = 128
VMEM_LIMIT = 56 * 1024 * 1024
NEG_BIG = -1e30
Q_SCALE = math.log2(math.e) / math.sqrt(HEAD_DIM)

FFN_TM, FFN_TF = 1024, 512
PROJ_TM = 256
GATE_TM, GATE_TN = 1024, 1792
MERGE_TM = 256
QB = 256
KC = 256
MW = 128
IDX_HPG = 1
BISECT_FIXED = 17
BISECT_STEPS = 3
HALO = 16


def _params(sem):
    return pltpu.CompilerParams(dimension_semantics=sem, vmem_limit_bytes=VMEM_LIMIT)


def _rms(x, g):
    ms = jnp.mean(x * x, axis=-1, keepdims=True)
    return x * lax.rsqrt(ms + EPS) * g


def _mm(a, b):
    return jnp.dot(a, b, preferred_element_type=jnp.float32)


def _ffn_kernel(emit_weights, h_ref, g_ref, wg_ref, wu_ref, wd_ref, o_ref, *rest):
    u_scr = rest[-1]
    j = pl.program_id(1)

    @pl.when(j == 0)
    def _():
        h = h_ref[...]
        u_scr[...] = _rms(h, g_ref[...]).astype(jnp.bfloat16)
        o_ref[...] = h

    bf = jnp.bfloat16
    wg, wu, wd = wg_ref[...].astype(bf), wu_ref[...].astype(bf), wd_ref[...].astype(bf)
    if emit_weights:
        for dst, w in zip(rest[:3], (wg, wu, wd)):
            dst[...] = w
    u = u_scr[...]
    gate = _mm(u, wg)
    up = _mm(u, wu)
    a = (gate * jax.nn.sigmoid(gate)) * (up * 0.5)
    o_ref[...] += _mm(a.astype(bf), wd)


def _ffn(h, g, wg, wu, wd, emit_weights=False):
    n = h.shape[0]
    tm = min(FFN_TM, n)
    assert not emit_weights or n == tm
    w_specs = [
        pl.BlockSpec((D_MODEL, FFN_TF), lambda i, j: (0, j)),
        pl.BlockSpec((D_MODEL, FFN_TF), lambda i, j: (0, j)),
        pl.BlockSpec((FFN_TF, D_MODEL), lambda i, j: (j, 0)),
    ]
    out_shape = [jax.ShapeDtypeStruct((n, D_MODEL), jnp.float32)]
    out_specs = [pl.BlockSpec((tm, D_MODEL), lambda i, j: (i, 0))]
    if emit_weights:
        out_shape += [jax.ShapeDtypeStruct(w.shape, jnp.bfloat16) for w in (wg, wu, wd)]
        out_specs += w_specs
    return pl.pallas_call(
        functools.partial(_ffn_kernel, emit_weights),
        out_shape=out_shape,
        grid=(n // tm, D_FF // FFN_TF),
        in_specs=[
            pl.BlockSpec((tm, D_MODEL), lambda i, j: (i, 0)),
            pl.BlockSpec((1, D_MODEL), lambda i, j: (0, 0)),
            *w_specs,
        ],
        out_specs=out_specs,
        scratch_shapes=[pltpu.VMEM((tm, D_MODEL), jnp.bfloat16)],
        compiler_params=_params(("parallel", "arbitrary")),
        name="ffn",
    )(h, g, wg, wu, wd)


def _rope(x, cos, s_lo, s_hi, half):
    n = x.shape[-1]
    return x * cos + pltpu.roll(x, n - half, 1) * s_lo + pltpu.roll(x, half, 1) * s_hi


def _proj_attn_kernel(h_ref, g_ref, wq_ref, wqi_ref, wk_ref, wv_ref, wkw_ref, gq_ref, gk_ref,
                      tab_ref, q_ref, qi_ref, k_ref, v_ref, ki_ref, wi_ref):
    bf = jnp.bfloat16
    u = _rms(h_ref[...], g_ref[...]).astype(bf)
    ca, sa_lo, sa_hi = tab_ref[0], tab_ref[1], tab_ref[2]
    ci, si_lo, si_hi = tab_ref[3], tab_ref[4], tab_ref[5]
    half_a = HEAD_DIM // ROT_DIV // 2
    half_i = IDX_DIM // ROT_DIV // 2
    gq = gq_ref[...]
    gk = gk_ref[...]
    z = _dot_nt(u, wq_ref[...])
    for hd in range(N_HEADS):
        sl = slice(hd * LANES, (hd + 1) * LANES)
        q = _rope(_rms(z[:, sl], gq), ca, sa_lo, sa_hi, half_a)
        q_ref[:, sl] = (q * Q_SCALE).astype(bf)
    z = _dot_nt(u, wqi_ref[...])
    for p in range(IDX_WIDTH // LANES):
        sl = slice(p * LANES, (p + 1) * LANES)
        qi_ref[:, sl] = _rope(z[:, sl], ci, si_lo, si_hi, half_i).astype(bf)
    z = _dot_nt(u, wk_ref[...])
    for hd in range(N_KV_HEADS):
        sl = slice(hd * LANES, (hd + 1) * LANES)
        k_ref[:, sl] = _rope(_rms(z[:, sl], gk), ca, sa_lo, sa_hi, half_a).astype(bf)
    v_ref[...] = _dot_nt(u, wv_ref[...]).astype(bf)
    x = _dot_nt(u, wkw_ref[...])
    lane = lax.broadcasted_iota(jnp.int32, x.shape, 1)
    is_ki = lane < IDX_DIM
    roped = _rope(x, jnp.where(is_ki, ci, 1.0), jnp.where(is_ki, si_lo, 0.0),
                  jnp.where(is_ki, si_hi, 0.0), half_i)
    swapped = pltpu.roll(roped, IDX_DIM, 1)
    ki_ref[...] = jnp.where(is_ki, roped, swapped).astype(bf)
    wi_ref[...] = swapped


def _proj_attn(h, g, wt, row_starts, gq, gk, tabs):
    n = h.shape[0]
    tm = min(PROJ_TM, n)
    nt = tabs.shape[1] // tm
    row = lambda i: (i, 0)
    const = lambda i: (0, 0)
    bf = jnp.bfloat16
    heights = (ATTN_WIDTH, IDX_WIDTH, KV_WIDTH, KV_WIDTH, LANES)
    assert all(r % 16 == 0 and r + hgt <= wt.shape[0] for r, hgt in zip(row_starts, heights))
    weight = lambda r0, hgt: pl.BlockSpec((pl.Element(hgt), pl.Element(D_MODEL)),
                                          lambda i: (r0, 0))
    return pl.pallas_call(
        _proj_attn_kernel,
        out_shape=(
            jax.ShapeDtypeStruct((n, ATTN_WIDTH), bf),
            jax.ShapeDtypeStruct((n, IDX_WIDTH), bf),
            jax.ShapeDtypeStruct((n, KV_WIDTH), bf),
            jax.ShapeDtypeStruct((n, KV_WIDTH), bf),
            jax.ShapeDtypeStruct((n, LANES), bf),
            jax.ShapeDtypeStruct((n, LANES), jnp.float32),
        ),
        grid=(n // tm,),
        in_specs=[
            pl.BlockSpec((tm, D_MODEL), row),
            pl.BlockSpec((1, D_MODEL), const),
            *[weight(r0, hgt) for r0, hgt in zip(row_starts, heights)],
            pl.BlockSpec((1, LANES), const),
            pl.BlockSpec((1, LANES), const),
            pl.BlockSpec((6, tm, LANES), lambda i: (0, i % nt, 0)),
        ],
        out_specs=(
            pl.BlockSpec((tm, ATTN_WIDTH), row),
            pl.BlockSpec((tm, IDX_WIDTH), row),
            pl.BlockSpec((tm, KV_WIDTH), row),
            pl.BlockSpec((tm, KV_WIDTH), row),
            pl.BlockSpec((tm, LANES), row),
            pl.BlockSpec((tm, LANES), row),
        ),
        compiler_params=_params(("parallel",)),
        name="proj_attn",
    )(h, g, wt, wt, wt, wt, wt, gq, gk, tabs)


def _proj_gate_kernel(h_ref, g_ref, w_ref, o_ref, u_scr):
    @pl.when(pl.program_id(1) == 0)
    def _():
        u_scr[...] = _rms(h_ref[...], g_ref[...]).astype(jnp.bfloat16)

    o_ref[...] = _dot_nt(u_scr[...], w_ref[...]).astype(o_ref.dtype)


def _proj_gate(h, g, wt, row0, cols):
    n = h.shape[0]
    tm = min(GATE_TM, n)
    cols = -(-cols // GATE_TN) * GATE_TN
    assert row0 + cols <= wt.shape[0] and row0 % 16 == 0
    return pl.pallas_call(
        _proj_gate_kernel,
        out_shape=jax.ShapeDtypeStruct((n, cols), jnp.bfloat16),
        grid=(n // tm, cols // GATE_TN),
        in_specs=[
            pl.BlockSpec((tm, D_MODEL), lambda i, j: (i, 0)),
            pl.BlockSpec((1, D_MODEL), lambda i, j: (0, 0)),
            pl.BlockSpec((pl.Element(GATE_TN), pl.Element(D_MODEL)),
                         lambda i, j: (pl.multiple_of(row0 + j * GATE_TN, 16), 0)),
        ],
        out_specs=pl.BlockSpec((tm, GATE_TN), lambda i, j: (i, j)),
        scratch_shapes=[pltpu.VMEM((tm, D_MODEL), jnp.bfloat16)],
        compiler_params=_params(("parallel", "arbitrary")),
        name="proj_gate",
    )(h, g, wt)


def _dot_nt(a, b):
    return lax.dot_general(a, b, (((1,), (1,)), ((), ())), preferred_element_type=jnp.float32)


def _attn_kernel(k_sel, n_cast, q_ref, qi_ref, wi_ref, k_ref, v_ref, ki_ref, km_ref, vm_ref,
                 kim_ref, *rest):
    cast_in, o_ref, cast_out = rest[:n_cast], rest[n_cast], rest[n_cast + 1:2 * n_cast + 1]
    (qs_scr, qis_scr, sctm_scr, sct_scr, biasm_scr, bias_scr,
     m_scr, l_scr, acc_scr) = rest[2 * n_cast + 1:]
    for src, dst in zip(cast_in, cast_out):
        dst[...] = src[...].astype(dst.dtype)

    qb = pl.program_id(1)
    n_seq = qb + 1
    rep = N_HEADS // N_KV_HEADS
    hpg = IDX_HPG
    lane_q = lax.broadcasted_iota(jnp.int32, (QB, LANES), 1)
    neg_inf = float("-inf")

    for g in range(N_KV_HEADS):
        for r in range(rep):
            hd = g * rep + r
            qs_scr[g, r * QB:(r + 1) * QB, :] = q_ref[0, :, hd * LANES:(hd + 1) * LANES]
    for i in range(N_IDX_HEADS // hpg):
        for r in range(hpg):
            hd = i * hpg + r
            pair = qi_ref[0, :, (hd // 2) * LANES:(hd // 2 + 1) * LANES]
            keep = (lane_q < IDX_DIM) if hd % 2 == 0 else (lane_q >= IDX_DIM)
            qis_scr[i, r * QB:(r + 1) * QB, :] = jnp.where(keep, pair, jnp.zeros_like(pair))

    w_all = wi_ref[0]

    def index_scores(ki_c):
        kw = ki_c.shape[0]
        score = jnp.zeros((QB, kw), jnp.float32)
        for i in range(N_IDX_HEADS // hpg):
            d = _dot_nt(qis_scr[i], ki_c)
            for r in range(hpg):
                hd = i * hpg + r
                score = score + jnp.maximum(d[r * QB:(r + 1) * QB], 0.0) * w_all[:, hd:hd + 1]
        return score

    lane_m = lax.broadcasted_iota(jnp.int32, (QB, MW), 1)
    sctm_scr[...] = jnp.where(lane_m < N_META, index_scores(kim_ref[...]), neg_inf).T
    row = lax.broadcasted_iota(jnp.int32, (QB, KC), 0)
    lane = lax.broadcasted_iota(jnp.int32, (QB, KC), 1)

    def score_chunk(c):
        ki_c = ki_ref[0, pl.ds(pl.multiple_of(c * KC, KC), KC), :]
        causal = (c - qb) * KC + lane <= row
        s = jnp.where(causal, index_scores(ki_c), neg_inf)
        bias_scr[c] = s
        sct_scr[c] = s.T

    def score_pair(i, carry):
        score_chunk(2 * i)
        score_chunk(2 * i + 1)
        return carry

    lax.fori_loop(0, n_seq // 2, score_pair, 0)

    @pl.when(n_seq % 2 == 1)
    def _():
        score_chunk(n_seq - 1)
        sct_scr[n_seq] = jnp.full((KC, QB), neg_inf, jnp.float32)

    def over_keys(f, init):
        def pair(i, a):
            c = 2 * i
            a = f(sct_scr[c], MW + c * KC, a)
            return f(sct_scr[c + 1], MW + (c + 1) * KC, a)
        return lax.fori_loop(0, (n_seq + 1) // 2, pair, f(sctm_scr[...], 0, init))

    def fold8(x):
        return jnp.sum(x.reshape(x.shape[0] // 8, 8, QB), axis=0)

    def count(pred):
        part = over_keys(lambda x, p0, a: a + fold8(jnp.where(pred(x, p0), 1.0, 0.0)),
                         jnp.zeros((8, QB), jnp.float32))
        return jnp.sum(part, axis=0, keepdims=True)

    kf = float(k_sel)
    qlane = lax.broadcasted_iota(jnp.int32, (1, QB), 1)
    n_valid = (N_META + 1 + qb * QB + qlane).astype(jnp.float32)

    def minmax(x, p0, carry):
        mn, mx = carry
        x3 = x.reshape(x.shape[0] // 8, 8, QB)
        mx = jnp.maximum(mx, jnp.max(x3, axis=0))
        mn = jnp.minimum(mn, jnp.min(jnp.where(x3 == neg_inf, float("inf"), x3), axis=0))
        return mn, mx

    mn, mx = over_keys(minmax, (jnp.full((8, QB), float("inf"), jnp.float32),
                                jnp.full((8, QB), neg_inf, jnp.float32)))
    mn = jnp.min(mn, axis=0, keepdims=True)
    mx = jnp.max(mx, axis=0, keepdims=True)

    def count_ge(t):
        return count(lambda x, p0: x >= t)

    def midpoint(lo, hi):
        return lo + (hi - lo) * 0.5

    def is_active(lo, hi, cnt_lo):
        mid = midpoint(lo, hi)
        return jnp.where((cnt_lo != kf) & (mid > lo) & (mid < hi), 1.0, 0.0)

    cnt_mx = count_ge(mx)
    at_max = cnt_mx >= kf
    lo0 = jnp.where(at_max, mx, mn)
    cnt0 = jnp.where(at_max, cnt_mx, n_valid)
    act0 = jnp.where(n_valid > kf, is_active(lo0, mx, cnt0), 0.0)

    def bisect_once(state):
        lo, hi, cnt_lo, act = state
        mid = midpoint(lo, hi)
        cnt = count_ge(mid)
        up = (act > 0.5) & (cnt >= kf)
        down = (act > 0.5) & (cnt < kf)
        lo = jnp.where(up, mid, lo)
        cnt_lo = jnp.where(up, cnt, cnt_lo)
        hi = jnp.where(down, mid, hi)
        return lo, hi, cnt_lo, act * is_active(lo, hi, cnt_lo)

    def bisect_trip(st):
        for _ in range(BISECT_STEPS):
            st = bisect_once(st)
        return st

    st = lax.fori_loop(0, BISECT_FIXED, lambda _, st: bisect_once(st), (lo0, mx, cnt0, act0))
    thr, _, cnt_thr, _ = lax.while_loop(lambda st: jnp.max(st[3]) > 0.5, bisect_trip, st)
    has_tie = cnt_thr > kf
    any_tie = jnp.max(jnp.where(has_tie, 1.0, 0.0)) > 0.5

    def write_bias(keep):
        biasm_scr[...] = jnp.where(keep(sctm_scr[...], 0), 0.0, NEG_BIG).T

        def body(c, carry):
            bias_scr[c] = jnp.where(keep(sct_scr[c], MW + c * KC), 0.0, NEG_BIG).T
            return carry
        lax.fori_loop(0, n_seq, body, 0)

    @pl.when(jnp.logical_not(any_tie))
    def _():
        biasm_scr[...] = jnp.where(sctm_scr[...] >= thr, 0.0, NEG_BIG).T
        thr_q = jnp.broadcast_to(thr, (LANES, QB)).T
        thr_q = jnp.concatenate([thr_q] * (KC // LANES), axis=1)

        def body(c, carry):
            bias_scr[c] = jnp.where(bias_scr[c] >= thr_q, 0.0, NEG_BIG)
            return carry
        lax.fori_loop(0, n_seq, body, 0)

    @pl.when(any_tie)
    def _():
        pos_bits = (MW + k_ref.shape[1] - 1).bit_length()
        far = 1 << pos_bits

        def tie_pos(x, p0):
            pos = p0 + lax.broadcasted_iota(jnp.int32, x.shape, 0)
            return jnp.where(x == thr, pos, far)

        need = kf - count(lambda x, p0: x > thr)

        def pos_step(i, last):
            cand = last | lax.shift_left(jnp.int32(1), pos_bits - 1 - i)
            below = count(lambda x, p0: tie_pos(x, p0) < cand)
            return jnp.where(below < need, cand, last)

        last = lax.fori_loop(0, pos_bits, pos_step, jnp.zeros((1, QB), jnp.int32))
        last = jnp.where(has_tie, last, far - 1)
        write_bias(lambda x, p0: jnp.where(x > thr, 1.0,
                                           jnp.where(tie_pos(x, p0) <= last, 1.0, 0.0)) > 0.5)

    m_scr[...] = jnp.full(m_scr.shape, NEG_BIG, jnp.float32)
    l_scr[...] = jnp.zeros(l_scr.shape, jnp.float32)
    acc_scr[...] = jnp.zeros(acc_scr.shape, jnp.float32)

    def attend(bias, k_c, v_c):
        kw = k_c.shape[0]
        for g in range(N_KV_HEADS):
            sl = slice(g * LANES, (g + 1) * LANES)
            s = _dot_nt(qs_scr[g], k_c[:, sl])
            s = jnp.concatenate([s[r * QB:(r + 1) * QB] + bias for r in range(rep)], axis=0)
            m_prev = m_scr[g]
            m_new = jnp.maximum(m_prev, jnp.max(s, axis=-1, keepdims=True))
            alpha = jnp.exp2(m_prev - m_new)
            p = jnp.exp2(s - jnp.concatenate([m_new] * (kw // LANES), axis=1))
            l_scr[g] = alpha * l_scr[g] + jnp.sum(p, axis=-1, keepdims=True)
            acc_scr[g] = alpha * acc_scr[g] + _mm(p.astype(jnp.bfloat16), v_c[:, sl])
            m_scr[g] = m_new

    attend(biasm_scr[...], km_ref[...], vm_ref[...])

    def attend_pair(i, carry):
        c = 2 * i
        rows = pl.ds(pl.multiple_of(c * KC, 2 * KC), 2 * KC)
        attend(jnp.concatenate([bias_scr[c], bias_scr[c + 1]], axis=1),
               k_ref[0, rows, :], v_ref[0, rows, :])
        return carry

    lax.fori_loop(0, n_seq // 2, attend_pair, 0)

    @pl.when(n_seq % 2 == 1)
    def _():
        c = n_seq - 1
        rows = pl.ds(pl.multiple_of(c * KC, KC), KC)
        attend(bias_scr[c], k_ref[0, rows, :], v_ref[0, rows, :])

    for g in range(N_KV_HEADS):
        o = acc_scr[g] / l_scr[g]
        for r in range(rep):
            hd = g * rep + r
            o_ref[0, :, hd * LANES:(hd + 1) * LANES] = o[r * QB:(r + 1) * QB].astype(o_ref.dtype)


def _cast_spec(w, nb, nq):
    rows, cols = w.shape
    if rows % (nb * nq * 16) == 0:
        tr = rows // (nb * nq)
        return pl.BlockSpec((tr, cols), lambda bi, qi_: (bi * nq + qi_, 0))
    assert rows % (nb * 16) == 0 and cols % (nq * LANES) == 0, w.shape
    return pl.BlockSpec((rows // nb, cols // nq), lambda bi, qi_: (bi, qi_))


def _attention(q, qi, wi, k, v, ki, km, vm, kim, k_sel, cast_weights):
    b, s, _ = q.shape
    nq = s // QB
    blk = lambda bi, qi_: (bi, qi_, 0)
    full = lambda bi, qi_: (bi, 0, 0)
    const = lambda bi, qi_: (0, 0)
    rep = N_HEADS // N_KV_HEADS
    f32 = jnp.float32
    cast_specs = [_cast_spec(w, b, nq) for w in cast_weights]
    return pl.pallas_call(
        functools.partial(_attn_kernel, k_sel, len(cast_weights)),
        out_shape=(jax.ShapeDtypeStruct((b, s, ATTN_WIDTH), jnp.bfloat16),
                   *[jax.ShapeDtypeStruct(w.shape, jnp.bfloat16) for w in cast_weights]),
        grid=(b, nq),
        in_specs=[
            pl.BlockSpec((1, QB, ATTN_WIDTH), blk),
            pl.BlockSpec((1, QB, IDX_WIDTH), blk),
            pl.BlockSpec((1, QB, LANES), blk),
            pl.BlockSpec((1, s, KV_WIDTH), full),
            pl.BlockSpec((1, s, KV_WIDTH), full),
            pl.BlockSpec((1, s, LANES), full),
            pl.BlockSpec((MW, KV_WIDTH), const),
            pl.BlockSpec((MW, KV_WIDTH), const),
            pl.BlockSpec((MW, LANES), const),
            *cast_specs,
        ],
        out_specs=(pl.BlockSpec((1, QB, ATTN_WIDTH), blk), *cast_specs),
        scratch_shapes=[
            pltpu.VMEM((N_KV_HEADS, rep * QB, LANES), jnp.bfloat16),
            pltpu.VMEM((N_IDX_HEADS // IDX_HPG, IDX_HPG * QB, LANES), jnp.bfloat16),
            pltpu.VMEM((MW, QB), f32),
            pltpu.VMEM((s // KC, KC, QB), f32),
            pltpu.VMEM((QB, MW), f32),
            pltpu.VMEM((s // KC, QB, KC), f32),
            pltpu.VMEM((N_KV_HEADS, rep * QB, LANES), f32),
            pltpu.VMEM((N_KV_HEADS, rep * QB, LANES), f32),
            pltpu.VMEM((N_KV_HEADS, rep * QB, LANES), f32),
        ],
        compiler_params=_params(("parallel", "arbitrary")),
        name="dsa_attention",
    )(q, qi, wi, k, v, ki, km, vm, kim, *cast_weights)


def _merge_kernel(tiles_per_seq, h_ref, attn_ref, xc_ref, gb_ref, cg_ref, ga0_ref, ga1_ref,
                  gc0_ref, gc1_ref, xc_prev_ref, cg_prev_ref, xc_meta_ref, cg_meta_ref,
                  cw_ref, cb_ref, wa_ref, wc_ref, wo_ref, o_ref):
    f32 = jnp.float32
    first = pl.program_id(0) % tiles_per_seq == 0
    gx = cg_ref[...].astype(f32) * xc_ref[...].astype(f32)
    halo_prev = cg_prev_ref[...].astype(f32) * xc_prev_ref[...].astype(f32)
    halo_meta = cg_meta_ref[...].astype(f32) * xc_meta_ref[...].astype(f32)
    halo = jnp.where(first, halo_meta, halo_prev)
    h1, h2 = halo[HALO - 1:HALO], halo[HALO - 2:HALO - 1]
    row = lax.broadcasted_iota(jnp.int32, gx.shape, 0)
    prev1 = jnp.where(row == 0, h1, pltpu.roll(gx, 1, 0))
    prev2 = jnp.where(row == 0, h2, jnp.where(row == 1, h1, pltpu.roll(gx, 2, 0)))
    cw = cw_ref[...]
    conv = cw[0:1] * prev2 + cw[1:2] * prev1 + cw[2:3] * gx + cb_ref[...]
    feat = (gb_ref[...].astype(f32) * conv).astype(jnp.bfloat16)
    y_conv = _mm(feat, wc_ref[...])
    y_attn = _mm(attn_ref[...], wa_ref[...])
    half = D_MODEL // 2
    sig = lambda r: jax.nn.sigmoid(r[...].astype(f32))
    merged = jnp.concatenate(
        [sig(ga0_ref) * y_attn[:, :half] + sig(gc0_ref) * y_conv[:, :half],
         sig(ga1_ref) * y_attn[:, half:] + sig(gc1_ref) * y_conv[:, half:]], axis=1)
    o_ref[...] = h_ref[...] + _mm(merged.astype(jnp.bfloat16), wo_ref[...])


def _merge(h, attn, zg, zg_meta, conv_w, conv_b, wa, wc, wo, seq):
    n = h.shape[0]
    tm = MERGE_TM
    tiles_per_seq = seq // tm
    row2 = lambda i: (i, 0)
    const = lambda i: (0, 0)
    col = lambda c: pl.BlockSpec((tm, CONV_WIDTH), lambda i: (i, c))
    prev = lambda c: pl.BlockSpec((HALO, CONV_WIDTH),
                                  lambda i: (jnp.maximum(i * (tm // HALO) - 1, 0), c))
    meta = lambda c: pl.BlockSpec((HALO, CONV_WIDTH), lambda i: (N_META // HALO - 1, c))
    single = pl.Buffered(1)
    return pl.pallas_call(
        functools.partial(_merge_kernel, tiles_per_seq),
        out_shape=jax.ShapeDtypeStruct((n, D_MODEL), jnp.float32),
        grid=(n // tm,),
        in_specs=[
            pl.BlockSpec((tm, D_MODEL), row2),
            pl.BlockSpec((tm, ATTN_WIDTH), row2),
            col(0), col(1), col(2), col(3), col(4), col(5), col(6),
            prev(0), prev(2), meta(0), meta(2),
            pl.BlockSpec((CONV_K, CONV_WIDTH), const),
            pl.BlockSpec((1, CONV_WIDTH), const),
            pl.BlockSpec((ATTN_WIDTH, D_MODEL), const, pipeline_mode=single),
            pl.BlockSpec((CONV_WIDTH, D_MODEL), const, pipeline_mode=single),
            pl.BlockSpec((D_MODEL, D_MODEL), const, pipeline_mode=single),
        ],
        out_specs=pl.BlockSpec((tm, D_MODEL), row2),
        compiler_params=_params(("parallel",)),
        name="merge",
    )(h, attn, zg, zg, zg, zg, zg, zg, zg, zg, zg, zg_meta, zg_meta, conv_w, conv_b, wa, wc, wo)


def _rope_tables(pos, head_dim):
    rot = head_dim // ROT_DIV
    half = rot // 2
    inv = ROPE_THETA ** (-jnp.arange(0, rot, 2, dtype=jnp.float32) / rot)
    ang = pos.astype(jnp.float32)[:, None] * inv[None, :]
    cos, sin = jnp.cos(ang), jnp.sin(ang)
    n = pos.shape[0]
    pad = head_dim - rot
    c = jnp.concatenate([cos, cos, jnp.ones((n, pad), jnp.float32)], axis=1)
    zeros_h = jnp.zeros((n, half), jnp.float32)
    zeros_p = jnp.zeros((n, pad), jnp.float32)
    s_lo = jnp.concatenate([-sin, zeros_h, zeros_p], axis=1)
    s_hi = jnp.concatenate([zeros_h, sin, zeros_p], axis=1)
    tile = LANES // head_dim
    return [jnp.tile(t, (1, tile)) for t in (c, s_lo, s_hi)]


def kernel(x, meta_tokens, ffn1_norm_g, ffn1_w_gate, ffn1_w_up, ffn1_w_down, mix_norm_g, w_in, q_norm_g, k_norm_g, conv_w, conv_b, w_attn_branch, w_conv_branch, w_out, ffn2_norm_g, ffn2_w_gate, ffn2_w_up, ffn2_w_down):
    bsz, seq, d = x.shape
    bf = jnp.bfloat16
    k_sel = min(TOPK_MAX, seq // 4)
    assert d == D_MODEL and seq % QB == 0 and k_sel <= KC

    widths = [ATTN_WIDTH, KV_WIDTH, KV_WIDTH, IDX_WIDTH, IDX_DIM, N_IDX_HEADS,
              CONV_WIDTH, CONV_WIDTH, CONV_WIDTH, D_MODEL, D_MODEL]
    offs = [int(o) for o in np.concatenate([[0], np.cumsum(widths)])]
    attn_rows = (offs[0], offs[3], offs[1], offs[2], offs[4])

    h = x.reshape(bsz * seq, d)
    hm = meta_tokens.astype(x.dtype)
    pos_seq = jnp.arange(N_META, N_META + seq, dtype=jnp.int32)
    pos_meta = jnp.arange(N_META, dtype=jnp.int32)
    tabs_seq = jnp.stack(_rope_tables(pos_seq, HEAD_DIM) + _rope_tables(pos_seq, IDX_DIM))
    tabs_meta = jnp.stack(_rope_tables(pos_meta, HEAD_DIM) + _rope_tables(pos_meta, IDX_DIM))

    for l in range(ffn1_norm_g.shape[0]):
        wl = w_in[l].T.astype(bf)
        g1 = ffn1_norm_g[l][None]
        gm = mix_norm_g[l][None]

        hm, wg1, wu1, wd1 = _ffn(hm, g1, ffn1_w_gate[l], ffn1_w_up[l], ffn1_w_down[l],
                                 emit_weights=True)
        h, = _ffn(h, g1, wg1, wu1, wd1)

        gq, gk = q_norm_g[l][None], k_norm_g[l][None]
        q, qi, k, v, ki, wi = _proj_attn(h, gm, wl, attn_rows, gq, gk, tabs_seq)
        _, _, km, vm, kim, _ = _proj_attn(hm, gm, wl, attn_rows, gq, gk, tabs_meta)
        zg = _proj_gate(h, gm, wl, offs[6], offs[11] - offs[6])
        zg_meta = _proj_gate(hm, gm, wl, offs[6], 3 * CONV_WIDTH)

        pad_rows = lambda a: jnp.pad(a, ((0, MW - N_META), (0, 0)))
        r3 = lambda a: a.reshape(bsz, seq, a.shape[-1])
        later = [w_attn_branch[l], w_conv_branch[l], w_out[l],
                 ffn2_w_gate[l], ffn2_w_up[l], ffn2_w_down[l]]
        attn, wa, wc, wo, wg2, wu2, wd2 = _attention(
            r3(q), r3(qi), r3(wi), r3(k), r3(v), r3(ki),
            pad_rows(km), pad_rows(vm), pad_rows(kim), k_sel, later)
        attn = attn.reshape(bsz * seq, ATTN_WIDTH)

        h = _merge(h, attn, zg, zg_meta, conv_w[l], conv_b[l][None], wa, wc, wo, seq)

        h, = _ffn(h, ffn2_norm_g[l][None], wg2, wu2, wd2)

    return h.reshape(bsz, seq, d)
```

```python
import functools
import math

import jax
import jax.numpy as jnp
import numpy as np
from jax import lax
from jax.experimental import pallas as pl
from jax.experimental.pallas import tpu as pltpu

D_MODEL = 2048
N_META = 16
N_HEADS = 8
N_KV_HEADS = 2
HEAD_DIM = 128
ATTN_WIDTH = N_HEADS * HEAD_DIM
KV_WIDTH = N_KV_HEADS * HEAD_DIM
N_IDX_HEADS = 16
IDX_DIM = 64
IDX_WIDTH = N_IDX_HEADS * IDX_DIM
TOPK_MAX = 256
CONV_WIDTH = D_MODEL // 2
CONV_K = 3
D_FF = 5632
ROPE_THETA = 500000.0
ROT_DIV = 4
EPS = 1e-6

LANES = 128
VMEM_LIMIT = 56 * 1024 * 1024
NEG_BIG = -1e30
Q_SCALE = math.log2(math.e) / math.sqrt(HEAD_DIM)

FFN_TM, FFN_TF = 1024, 512
PROJ_TM = 256
GATE_TM, GATE_TN = 1024, 1792
MERGE_TM = 256
QB = 256
KC = 256
MW = 128
IDX_HPG = 1
BISECT_FIXED = 17
BISECT_STEPS = 3
HALO = 16


def _params(sem):
    return pltpu.CompilerParams(dimension_semantics=sem, vmem_limit_bytes=VMEM_LIMIT)


def _rms(x, g):
    ms = jnp.mean(x * x, axis=-1, keepdims=True)
    return x * lax.rsqrt(ms + EPS) * g


def _mm(a, b):
    return jnp.dot(a, b, preferred_element_type=jnp.float32)


def _ffn_kernel(emit_weights, h_ref, g_ref, wg_ref, wu_ref, wd_ref, o_ref, *rest):
    u_scr = rest[-1]
    j = pl.program_id(1)

    @pl.when(j == 0)
    def _():
        h = h_ref[...]
        u_scr[...] = _rms(h, g_ref[...]).astype(jnp.bfloat16)
        o_ref[...] = h

    bf = jnp.bfloat16
    wg, wu, wd = wg_ref[...].astype(bf), wu_ref[...].astype(bf), wd_ref[...].astype(bf)
    if emit_weights:
        for dst, w in zip(rest[:3], (wg, wu, wd)):
            dst[...] = w
    u = u_scr[...]
    gate = _mm(u, wg)
    up = _mm(u, wu)
    a = (gate * jax.nn.sigmoid(gate)) * (up * 0.5)
    o_ref[...] += _mm(a.astype(bf), wd)


def _ffn_nested_kernel(h_ref, g_ref, wg_hbm, wu_hbm, wd_hbm, o_ref, u_scr):
    bf = jnp.bfloat16
    h = h_ref[...]
    u_scr[...] = _rms(h, g_ref[...]).astype(bf)
    o_ref[...] = h

    def step(wg_ref, wu_ref, wd_ref):
        u = u_scr[...]
        gate = _mm(u, wg_ref[...])
        up = _mm(u, wu_ref[...])
        a = (gate * jax.nn.sigmoid(gate)) * (up * 0.5)
        o_ref[...] += _mm(a.astype(bf), wd_ref[...])

    pltpu.emit_pipeline(
        step,
        grid=(D_FF // FFN_TF,),
        in_specs=[
            pl.BlockSpec((D_MODEL, FFN_TF), lambda j: (0, j)),
            pl.BlockSpec((D_MODEL, FFN_TF), lambda j: (0, j)),
            pl.BlockSpec((FFN_TF, D_MODEL), lambda j: (j, 0)),
        ],
        trace_scopes=False,
    )(wg_hbm, wu_hbm, wd_hbm)


def _ffn_nested(h, g, wg, wu, wd):
    n = h.shape[0]
    tm = FFN_TM
    hbm = pl.BlockSpec(memory_space=pl.ANY)
    return pl.pallas_call(
        _ffn_nested_kernel,
        out_shape=[jax.ShapeDtypeStruct((n, D_MODEL), jnp.float32)],
        grid=(n // tm,),
        in_specs=[
            pl.BlockSpec((tm, D_MODEL), lambda i: (i, 0)),
            pl.BlockSpec((1, D_MODEL), lambda i: (0, 0)),
            hbm, hbm, hbm,
        ],
        out_specs=[pl.BlockSpec((tm, D_MODEL), lambda i: (i, 0))],
        scratch_shapes=[pltpu.VMEM((tm, D_MODEL), jnp.bfloat16)],
        compiler_params=_params(("arbitrary",)),
        name="ffn",
    )(h, g, wg, wu, wd)


def _ffn(h, g, wg, wu, wd, emit_weights=False):
    n = h.shape[0]
    if not emit_weights and n % FFN_TM == 0:
        return _ffn_nested(h, g, wg, wu, wd)
    tm = min(FFN_TM, n)
    assert not emit_weights or n == tm
    w_specs = [
        pl.BlockSpec((D_MODEL, FFN_TF), lambda i, j: (0, j)),
        pl.BlockSpec((D_MODEL, FFN_TF), lambda i, j: (0, j)),
        pl.BlockSpec((FFN_TF, D_MODEL), lambda i, j: (j, 0)),
    ]
    out_shape = [jax.ShapeDtypeStruct((n, D_MODEL), jnp.float32)]
    out_specs = [pl.BlockSpec((tm, D_MODEL), lambda i, j: (i, 0))]
    if emit_weights:
        out_shape += [jax.ShapeDtypeStruct(w.shape, jnp.bfloat16) for w in (wg, wu, wd)]
        out_specs += w_specs
    return pl.pallas_call(
        functools.partial(_ffn_kernel, emit_weights),
        out_shape=out_shape,
        grid=(n // tm, D_FF // FFN_TF),
        in_specs=[
            pl.BlockSpec((tm, D_MODEL), lambda i, j: (i, 0)),
            pl.BlockSpec((1, D_MODEL), lambda i, j: (0, 0)),
            *w_specs,
        ],
        out_specs=out_specs,
        scratch_shapes=[pltpu.VMEM((tm, D_MODEL), jnp.bfloat16)],
        compiler_params=_params(("parallel", "arbitrary")),
        name="ffn",
    )(h, g, wg, wu, wd)


def _rope(x, cos, s_lo, s_hi, half):
    n = x.shape[-1]
    return x * cos + pltpu.roll(x, n - half, 1) * s_lo + pltpu.roll(x, half, 1) * s_hi


def _proj_attn_kernel(h_ref, g_ref, wq_ref, wqi_ref, wk_ref, wv_ref, wkw_ref, gq_ref, gk_ref,
                      tab_ref, q_ref, qi_ref, k_ref, v_ref, ki_ref, wi_ref):
    bf = jnp.bfloat16
    u = _rms(h_ref[...], g_ref[...]).astype(bf)
    ca, sa_lo, sa_hi = tab_ref[0], tab_ref[1], tab_ref[2]
    ci, si_lo, si_hi = tab_ref[3], tab_ref[4], tab_ref[5]
    half_a = HEAD_DIM // ROT_DIV // 2
    half_i = IDX_DIM // ROT_DIV // 2
    gq = gq_ref[...]
    gk = gk_ref[...]
    z = _dot_nt(u, wq_ref[...])
    for hd in range(N_HEADS):
        sl = slice(hd * LANES, (hd + 1) * LANES)
        q = _rope(_rms(z[:, sl], gq), ca, sa_lo, sa_hi, half_a)
        q_ref[:, sl] = (q * Q_SCALE).astype(bf)
    z = _dot_nt(u, wqi_ref[...])
    for p in range(IDX_WIDTH // LANES):
        sl = slice(p * LANES, (p + 1) * LANES)
        qi_ref[:, sl] = _rope(z[:, sl], ci, si_lo, si_hi, half_i).astype(bf)
    z = _dot_nt(u, wk_ref[...])
    for hd in range(N_KV_HEADS):
        sl = slice(hd * LANES, (hd + 1) * LANES)
        k_ref[:, sl] = _rope(_rms(z[:, sl], gk), ca, sa_lo, sa_hi, half_a).astype(bf)
    v_ref[...] = _dot_nt(u, wv_ref[...]).astype(bf)
    x = _dot_nt(u, wkw_ref[...])
    lane = lax.broadcasted_iota(jnp.int32, x.shape, 1)
    is_ki = lane < IDX_DIM
    roped = _rope(x, jnp.where(is_ki, ci, 1.0), jnp.where(is_ki, si_lo, 0.0),
                  jnp.where(is_ki, si_hi, 0.0), half_i)
    swapped = pltpu.roll(roped, IDX_DIM, 1)
    ki_ref[...] = jnp.where(is_ki, roped, swapped).astype(bf)
    wi_ref[...] = swapped


def _proj_attn(h, g, wt, row_starts, gq, gk, tabs):
    n = h.shape[0]
    tm = min(PROJ_TM, n)
    nt = tabs.shape[1] // tm
    row = lambda i: (i, 0)
    const = lambda i: (0, 0)
    bf = jnp.bfloat16
    heights = (ATTN_WIDTH, IDX_WIDTH, KV_WIDTH, KV_WIDTH, LANES)
    assert all(r % 16 == 0 and r + hgt <= wt.shape[0] for r, hgt in zip(row_starts, heights))
    weight = lambda r0, hgt: pl.BlockSpec((pl.Element(hgt), pl.Element(D_MODEL)),
                                          lambda i: (r0, 0))
    return pl.pallas_call(
        _proj_attn_kernel,
        out_shape=(
            jax.ShapeDtypeStruct((n, ATTN_WIDTH), bf),
            jax.ShapeDtypeStruct((n, IDX_WIDTH), bf),
            jax.ShapeDtypeStruct((n, KV_WIDTH), bf),
            jax.ShapeDtypeStruct((n, KV_WIDTH), bf),
            jax.ShapeDtypeStruct((n, LANES), bf),
            jax.ShapeDtypeStruct((n, LANES), jnp.float32),
        ),
        grid=(n // tm,),
        in_specs=[
            pl.BlockSpec((tm, D_MODEL), row),
            pl.BlockSpec((1, D_MODEL), const),
            *[weight(r0, hgt) for r0, hgt in zip(row_starts, heights)],
            pl.BlockSpec((1, LANES), const),
            pl.BlockSpec((1, LANES), const),
            pl.BlockSpec((6, tm, LANES), lambda i: (0, i % nt, 0)),
        ],
        out_specs=(
            pl.BlockSpec((tm, ATTN_WIDTH), row),
            pl.BlockSpec((tm, IDX_WIDTH), row),
            pl.BlockSpec((tm, KV_WIDTH), row),
            pl.BlockSpec((tm, KV_WIDTH), row),
            pl.BlockSpec((tm, LANES), row),
            pl.BlockSpec((tm, LANES), row),
        ),
        compiler_params=_params(("parallel",)),
        name="proj_attn",
    )(h, g, wt, wt, wt, wt, wt, gq, gk, tabs)


def _proj_gate_kernel(h_ref, g_ref, w_ref, o_ref, u_scr):
    @pl.when(pl.program_id(1) == 0)
    def _():
        u_scr[...] = _rms(h_ref[...], g_ref[...]).astype(jnp.bfloat16)

    o_ref[...] = _dot_nt(u_scr[...], w_ref[...]).astype(o_ref.dtype)


def _proj_gate(h, g, wt, row0, cols):
    n = h.shape[0]
    tm = min(GATE_TM, n)
    cols = -(-cols // GATE_TN) * GATE_TN
    assert row0 + cols <= wt.shape[0] and row0 % 16 == 0
    return pl.pallas_call(
        _proj_gate_kernel,
        out_shape=jax.ShapeDtypeStruct((n, cols), jnp.bfloat16),
        grid=(n // tm, cols // GATE_TN),
        in_specs=[
            pl.BlockSpec((tm, D_MODEL), lambda i, j: (i, 0)),
            pl.BlockSpec((1, D_MODEL), lambda i, j: (0, 0)),
            pl.BlockSpec((pl.Element(GATE_TN), pl.Element(D_MODEL)),
                         lambda i, j: (pl.multiple_of(row0 + j * GATE_TN, 16), 0)),
        ],
        out_specs=pl.BlockSpec((tm, GATE_TN), lambda i, j: (i, j)),
        scratch_shapes=[pltpu.VMEM((tm, D_MODEL), jnp.bfloat16)],
        compiler_params=_params(("parallel", "arbitrary")),
        name="proj_gate",
    )(h, g, wt)


def _dot_nt(a, b):
    return lax.dot_general(a, b, (((1,), (1,)), ((), ())), preferred_element_type=jnp.float32)


def _attn_kernel(k_sel, n_cast, q_ref, qi_ref, wi_ref, k_ref, v_ref, ki_ref, km_ref, vm_ref,
                 kim_ref, *rest):
    cast_in, o_ref, cast_out = rest[:n_cast], rest[n_cast], rest[n_cast + 1:2 * n_cast + 1]
    (qs_scr, qis_scr, sctm_scr, sct_scr, biasm_scr, bias_scr,
     m_scr, l_scr, acc_scr) = rest[2 * n_cast + 1:]
    for src, dst in zip(cast_in, cast_out):
        dst[...] = src[...].astype(dst.dtype)

    qb = pl.program_id(1)
    n_seq = qb + 1
    rep = N_HEADS // N_KV_HEADS
    hpg = IDX_HPG
    lane_q = lax.broadcasted_iota(jnp.int32, (QB, LANES), 1)
    neg_inf = float("-inf")

    for g in range(N_KV_HEADS):
        for r in range(rep):
            hd = g * rep + r
            qs_scr[g, r * QB:(r + 1) * QB, :] = q_ref[0, :, hd * LANES:(hd + 1) * LANES]
    for i in range(N_IDX_HEADS // hpg):
        for r in range(hpg):
            hd = i * hpg + r
            pair = qi_ref[0, :, (hd // 2) * LANES:(hd // 2 + 1) * LANES]
            keep = (lane_q < IDX_DIM) if hd % 2 == 0 else (lane_q >= IDX_DIM)
            qis_scr[i, r * QB:(r + 1) * QB, :] = jnp.where(keep, pair, jnp.zeros_like(pair))

    w_all = wi_ref[0]

    def index_scores(ki_c):
        kw = ki_c.shape[0]
        score = jnp.zeros((QB, kw), jnp.float32)
        for i in range(N_IDX_HEADS // hpg):
            d = _dot_nt(qis_scr[i], ki_c)
            for r in range(hpg):
                hd = i * hpg + r
                score = score + jnp.maximum(d[r * QB:(r + 1) * QB], 0.0) * w_all[:, hd:hd + 1]
        return score

    lane_m = lax.broadcasted_iota(jnp.int32, (QB, MW), 1)
    sctm_scr[...] = jnp.where(lane_m < N_META, index_scores(kim_ref[...]), neg_inf).T
    row = lax.broadcasted_iota(jnp.int32, (QB, KC), 0)
    lane = lax.broadcasted_iota(jnp.int32, (QB, KC), 1)

    def score_chunk(c):
        ki_c = ki_ref[0, pl.ds(pl.multiple_of(c * KC, KC), KC), :]
        causal = (c - qb) * KC + lane <= row
        s = jnp.where(causal, index_scores(ki_c), neg_inf)
        bias_scr[c] = s
        sct_scr[c] = s.T

    def score_pair(i, carry):
        score_chunk(2 * i)
        score_chunk(2 * i + 1)
        return carry

    lax.fori_loop(0, n_seq // 2, score_pair, 0)

    @pl.when(n_seq % 2 == 1)
    def _():
        score_chunk(n_seq - 1)
        sct_scr[n_seq] = jnp.full((KC, QB), neg_inf, jnp.float32)

    def over_keys(f, init):
        def pair(i, a):
            c = 2 * i
            a = f(sct_scr[c], MW + c * KC, a)
            return f(sct_scr[c + 1], MW + (c + 1) * KC, a)
        return lax.fori_loop(0, (n_seq + 1) // 2, pair, f(sctm_scr[...], 0, init))

    def fold8(x):
        return jnp.sum(x.reshape(x.shape[0] // 8, 8, QB), axis=0)

    def count(pred):
        part = over_keys(lambda x, p0, a: a + fold8(jnp.where(pred(x, p0), 1.0, 0.0)),
                         jnp.zeros((8, QB), jnp.float32))
        return jnp.sum(part, axis=0, keepdims=True)

    kf = float(k_sel)
    qlane = lax.broadcasted_iota(jnp.int32, (1, QB), 1)
    n_valid = (N_META + 1 + qb * QB + qlane).astype(jnp.float32)

    def minmax(x, p0, carry):
        mn, mx = carry
        x3 = x.reshape(x.shape[0] // 8, 8, QB)
        mx = jnp.maximum(mx, jnp.max(x3, axis=0))
        mn = jnp.minimum(mn, jnp.min(jnp.where(x3 == neg_inf, float("inf"), x3), axis=0))
        return mn, mx

    mn, mx = over_keys(minmax, (jnp.full((8, QB), float("inf"), jnp.float32),
                                jnp.full((8, QB), neg_inf, jnp.float32)))
    mn = jnp.min(mn, axis=0, keepdims=True)
    mx = jnp.max(mx, axis=0, keepdims=True)

    def count_ge(t):
        return count(lambda x, p0: x >= t)

    def midpoint(lo, hi):
        return lo + (hi - lo) * 0.5

    def is_active(lo, hi, cnt_lo):
        mid = midpoint(lo, hi)
        return jnp.where((cnt_lo != kf) & (mid > lo) & (mid < hi), 1.0, 0.0)

    cnt_mx = count_ge(mx)
    at_max = cnt_mx >= kf
    lo0 = jnp.where(at_max, mx, mn)
    cnt0 = jnp.where(at_max, cnt_mx, n_valid)
    act0 = jnp.where(n_valid > kf, is_active(lo0, mx, cnt0), 0.0)

    def bisect_once(state):
        lo, hi, cnt_lo, act = state
        mid = midpoint(lo, hi)
        cnt = count_ge(mid)
        up = (act > 0.5) & (cnt >= kf)
        down = (act > 0.5) & (cnt < kf)
        lo = jnp.where(up, mid, lo)
        cnt_lo = jnp.where(up, cnt, cnt_lo)
        hi = jnp.where(down, mid, hi)
        return lo, hi, cnt_lo, act * is_active(lo, hi, cnt_lo)

    def bisect_trip(st):
        for _ in range(BISECT_STEPS):
            st = bisect_once(st)
        return st

    st = lax.fori_loop(0, BISECT_FIXED, lambda _, st: bisect_once(st), (lo0, mx, cnt0, act0))
    thr, _, cnt_thr, _ = lax.while_loop(lambda st: jnp.max(st[3]) > 0.5, bisect_trip, st)
    has_tie = cnt_thr > kf
    any_tie = jnp.max(jnp.where(has_tie, 1.0, 0.0)) > 0.5

    def write_bias(keep):
        biasm_scr[...] = jnp.where(keep(sctm_scr[...], 0), 0.0, NEG_BIG).T

        def body(c, carry):
            bias_scr[c] = jnp.where(keep(sct_scr[c], MW + c * KC), 0.0, NEG_BIG).T
            return carry
        lax.fori_loop(0, n_seq, body, 0)

    @pl.when(jnp.logical_not(any_tie))
    def _():
        biasm_scr[...] = jnp.where(sctm_scr[...] >= thr, 0.0, NEG_BIG).T
        thr_q = jnp.broadcast_to(thr, (LANES, QB)).T
        thr_q = jnp.concatenate([thr_q] * (KC // LANES), axis=1)

        def body(c, carry):
            bias_scr[c] = jnp.where(bias_scr[c] >= thr_q, 0.0, NEG_BIG)
            return carry
        lax.fori_loop(0, n_seq, body, 0)

    @pl.when(any_tie)
    def _():
        pos_bits = (MW + k_ref.shape[1] - 1).bit_length()
        far = 1 << pos_bits

        def tie_pos(x, p0):
            pos = p0 + lax.broadcasted_iota(jnp.int32, x.shape, 0)
            return jnp.where(x == thr, pos, far)

        need = kf - count(lambda x, p0: x > thr)

        def pos_step(i, last):
            cand = last | lax.shift_left(jnp.int32(1), pos_bits - 1 - i)
            below = count(lambda x, p0: tie_pos(x, p0) < cand)
            return jnp.where(below < need, cand, last)

        last = lax.fori_loop(0, pos_bits, pos_step, jnp.zeros((1, QB), jnp.int32))
        last = jnp.where(has_tie, last, far - 1)
        write_bias(lambda x, p0: jnp.where(x > thr, 1.0,
                                           jnp.where(tie_pos(x, p0) <= last, 1.0, 0.0)) > 0.5)

    m_scr[...] = jnp.full(m_scr.shape, NEG_BIG, jnp.float32)
    l_scr[...] = jnp.zeros(l_scr.shape, jnp.float32)
    acc_scr[...] = jnp.zeros(acc_scr.shape, jnp.float32)

    def attend(bias, k_c, v_c):
        kw = k_c.shape[0]
        for g in range(N_KV_HEADS):
            sl = slice(g * LANES, (g + 1) * LANES)
            s = _dot_nt(qs_scr[g], k_c[:, sl])
            s = jnp.concatenate([s[r * QB:(r + 1) * QB] + bias for r in range(rep)], axis=0)
            m_prev = m_scr[g]
            m_new = jnp.maximum(m_prev, jnp.max(s, axis=-1, keepdims=True))
            alpha = jnp.exp2(m_prev - m_new)
            p = jnp.exp2(s - jnp.concatenate([m_new] * (kw // LANES), axis=1))
            l_scr[g] = alpha * l_scr[g] + jnp.sum(p, axis=-1, keepdims=True)
            acc_scr[g] = alpha * acc_scr[g] + _mm(p.astype(jnp.bfloat16), v_c[:, sl])
            m_scr[g] = m_new

    attend(biasm_scr[...], km_ref[...], vm_ref[...])

    def attend_pair(i, carry):
        c = 2 * i
        rows = pl.ds(pl.multiple_of(c * KC, 2 * KC), 2 * KC)
        attend(jnp.concatenate([bias_scr[c], bias_scr[c + 1]], axis=1),
               k_ref[0, rows, :], v_ref[0, rows, :])
        return carry

    lax.fori_loop(0, n_seq // 2, attend_pair, 0)

    @pl.when(n_seq % 2 == 1)
    def _():
        c = n_seq - 1
        rows = pl.ds(pl.multiple_of(c * KC, KC), KC)
        attend(bias_scr[c], k_ref[0, rows, :], v_ref[0, rows, :])

    for g in range(N_KV_HEADS):
        o = acc_scr[g] / l_scr[g]
        for r in range(rep):
            hd = g * rep + r
            o_ref[0, :, hd * LANES:(hd + 1) * LANES] = o[r * QB:(r + 1) * QB].astype(o_ref.dtype)


def _cast_spec(w, nb, nq):
    rows, cols = w.shape
    if rows % (nb * nq * 16) == 0:
        tr = rows // (nb * nq)
        return pl.BlockSpec((tr, cols), lambda bi, qi_: (bi * nq + qi_, 0))
    assert rows % (nb * 16) == 0 and cols % (nq * LANES) == 0, w.shape
    return pl.BlockSpec((rows // nb, cols // nq), lambda bi, qi_: (bi, qi_))


def _attention(q, qi, wi, k, v, ki, km, vm, kim, k_sel, cast_weights):
    b, s, _ = q.shape
    nq = s // QB
    blk = lambda bi, qi_: (bi, qi_, 0)
    full = lambda bi, qi_: (bi, 0, 0)
    const = lambda bi, qi_: (0, 0)
    rep = N_HEADS // N_KV_HEADS
    f32 = jnp.float32
    cast_specs = [_cast_spec(w, b, nq) for w in cast_weights]
    return pl.pallas_call(
        functools.partial(_attn_kernel, k_sel, len(cast_weights)),
        out_shape=(jax.ShapeDtypeStruct((b, s, ATTN_WIDTH), jnp.bfloat16),
                   *[jax.ShapeDtypeStruct(w.shape, jnp.bfloat16) for w in cast_weights]),
        grid=(b, nq),
        in_specs=[
            pl.BlockSpec((1, QB, ATTN_WIDTH), blk),
            pl.BlockSpec((1, QB, IDX_WIDTH), blk),
            pl.BlockSpec((1, QB, LANES), blk),
            pl.BlockSpec((1, s, KV_WIDTH), full),
            pl.BlockSpec((1, s, KV_WIDTH), full),
            pl.BlockSpec((1, s, LANES), full),
            pl.BlockSpec((MW, KV_WIDTH), const),
            pl.BlockSpec((MW, KV_WIDTH), const),
            pl.BlockSpec((MW, LANES), const),
            *cast_specs,
        ],
        out_specs=(pl.BlockSpec((1, QB, ATTN_WIDTH), blk), *cast_specs),
        scratch_shapes=[
            pltpu.VMEM((N_KV_HEADS, rep * QB, LANES), jnp.bfloat16),
            pltpu.VMEM((N_IDX_HEADS // IDX_HPG, IDX_HPG * QB, LANES), jnp.bfloat16),
            pltpu.VMEM((MW, QB), f32),
            pltpu.VMEM((s // KC, KC, QB), f32),
            pltpu.VMEM((QB, MW), f32),
            pltpu.VMEM((s // KC, QB, KC), f32),
            pltpu.VMEM((N_KV_HEADS, rep * QB, LANES), f32),
            pltpu.VMEM((N_KV_HEADS, rep * QB, LANES), f32),
            pltpu.VMEM((N_KV_HEADS, rep * QB, LANES), f32),
        ],
        compiler_params=_params(("parallel", "arbitrary")),
        name="dsa_attention",
    )(q, qi, wi, k, v, ki, km, vm, kim, *cast_weights)


def _merge_kernel(tiles_per_seq, h_ref, attn_ref, xc_ref, gb_ref, cg_ref, ga0_ref, ga1_ref,
                  gc0_ref, gc1_ref, xc_prev_ref, cg_prev_ref, xc_meta_ref, cg_meta_ref,
                  cw_ref, cb_ref, wa_ref, wc_ref, wo_ref, o_ref):
    f32 = jnp.float32
    first = pl.program_id(0) % tiles_per_seq == 0
    gx = cg_ref[...].astype(f32) * xc_ref[...].astype(f32)
    halo_prev = cg_prev_ref[...].astype(f32) * xc_prev_ref[...].astype(f32)
    halo_meta = cg_meta_ref[...].astype(f32) * xc_meta_ref[...].astype(f32)
    halo = jnp.where(first, halo_meta, halo_prev)
    h1, h2 = halo[HALO - 1:HALO], halo[HALO - 2:HALO - 1]
    row = lax.broadcasted_iota(jnp.int32, gx.shape, 0)
    prev1 = jnp.where(row == 0, h1, pltpu.roll(gx, 1, 0))
    prev2 = jnp.where(row == 0, h2, jnp.where(row == 1, h1, pltpu.roll(gx, 2, 0)))
    cw = cw_ref[...]
    conv = cw[0:1] * prev2 + cw[1:2] * prev1 + cw[2:3] * gx + cb_ref[...]
    feat = (gb_ref[...].astype(f32) * conv).astype(jnp.bfloat16)
    y_conv = _mm(feat, wc_ref[...])
    y_attn = _mm(attn_ref[...], wa_ref[...])
    half = D_MODEL // 2
    sig = lambda r: jax.nn.sigmoid(r[...].astype(f32))
    merged = jnp.concatenate(
        [sig(ga0_ref) * y_attn[:, :half] + sig(gc0_ref) * y_conv[:, :half],
         sig(ga1_ref) * y_attn[:, half:] + sig(gc1_ref) * y_conv[:, half:]], axis=1)
    o_ref[...] = h_ref[...] + _mm(merged.astype(jnp.bfloat16), wo_ref[...])


def _merge(h, attn, zg, zg_meta, conv_w, conv_b, wa, wc, wo, seq):
    n = h.shape[0]
    tm = MERGE_TM
    tiles_per_seq = seq // tm
    row2 = lambda i: (i, 0)
    const = lambda i: (0, 0)
    col = lambda c: pl.BlockSpec((tm, CONV_WIDTH), lambda i: (i, c))
    prev = lambda c: pl.BlockSpec((HALO, CONV_WIDTH),
                                  lambda i: (jnp.maximum(i * (tm // HALO) - 1, 0), c))
    meta = lambda c: pl.BlockSpec((HALO, CONV_WIDTH), lambda i: (N_META // HALO - 1, c))
    single = pl.Buffered(1)
    return pl.pallas_call(
        functools.partial(_merge_kernel, tiles_per_seq),
        out_shape=jax.ShapeDtypeStruct((n, D_MODEL), jnp.float32),
        grid=(n // tm,),
        in_specs=[
            pl.BlockSpec((tm, D_MODEL), row2),
            pl.BlockSpec((tm, ATTN_WIDTH), row2),
            col(0), col(1), col(2), col(3), col(4), col(5), col(6),
            prev(0), prev(2), meta(0), meta(2),
            pl.BlockSpec((CONV_K, CONV_WIDTH), const),
            pl.BlockSpec((1, CONV_WIDTH), const),
            pl.BlockSpec((ATTN_WIDTH, D_MODEL), const, pipeline_mode=single),
            pl.BlockSpec((CONV_WIDTH, D_MODEL), const, pipeline_mode=single),
            pl.BlockSpec((D_MODEL, D_MODEL), const, pipeline_mode=single),
        ],
        out_specs=pl.BlockSpec((tm, D_MODEL), row2),
        compiler_params=_params(("parallel",)),
        name="merge",
    )(h, attn, zg, zg, zg, zg, zg, zg, zg, zg, zg, zg_meta, zg_meta, conv_w, conv_b, wa, wc, wo)


def _rope_tables(pos, head_dim):
    rot = head_dim // ROT_DIV
    half = rot // 2
    inv = ROPE_THETA ** (-jnp.arange(0, rot, 2, dtype=jnp.float32) / rot)
    ang = pos.astype(jnp.float32)[:, None] * inv[None, :]
    cos, sin = jnp.cos(ang), jnp.sin(ang)
    n = pos.shape[0]
    pad = head_dim - rot
    c = jnp.concatenate([cos, cos, jnp.ones((n, pad), jnp.float32)], axis=1)
    zeros_h = jnp.zeros((n, half), jnp.float32)
    zeros_p = jnp.zeros((n, pad), jnp.float32)
    s_lo = jnp.concatenate([-sin, zeros_h, zeros_p], axis=1)
    s_hi = jnp.concatenate([zeros_h, sin, zeros_p], axis=1)
    tile = LANES // head_dim
    return [jnp.tile(t, (1, tile)) for t in (c, s_lo, s_hi)]


def kernel(x, meta_tokens, ffn1_norm_g, ffn1_w_gate, ffn1_w_up, ffn1_w_down, mix_norm_g, w_in, q_norm_g, k_norm_g, conv_w, conv_b, w_attn_branch, w_conv_branch, w_out, ffn2_norm_g, ffn2_w_gate, ffn2_w_up, ffn2_w_down):
    bsz, seq, d = x.shape
    bf = jnp.bfloat16
    k_sel = min(TOPK_MAX, seq // 4)
    assert d == D_MODEL and seq % QB == 0 and k_sel <= KC

    widths = [ATTN_WIDTH, KV_WIDTH, KV_WIDTH, IDX_WIDTH, IDX_DIM, N_IDX_HEADS,
              CONV_WIDTH, CONV_WIDTH, CONV_WIDTH, D_MODEL, D_MODEL]
    offs = [int(o) for o in np.concatenate([[0], np.cumsum(widths)])]
    attn_rows = (offs[0], offs[3], offs[1], offs[2], offs[4])

    h = x.reshape(bsz * seq, d)
    hm = meta_tokens.astype(x.dtype)
    pos_seq = jnp.arange(N_META, N_META + seq, dtype=jnp.int32)
    pos_meta = jnp.arange(N_META, dtype=jnp.int32)
    tabs_seq = jnp.stack(_rope_tables(pos_seq, HEAD_DIM) + _rope_tables(pos_seq, IDX_DIM))
    tabs_meta = jnp.stack(_rope_tables(pos_meta, HEAD_DIM) + _rope_tables(pos_meta, IDX_DIM))

    for l in range(ffn1_norm_g.shape[0]):
        wl = w_in[l].T.astype(bf)
        g1 = ffn1_norm_g[l][None]
        gm = mix_norm_g[l][None]

        hm, wg1, wu1, wd1 = _ffn(hm, g1, ffn1_w_gate[l], ffn1_w_up[l], ffn1_w_down[l],
                                 emit_weights=True)
        h, = _ffn(h, g1, wg1, wu1, wd1)

        gq, gk = q_norm_g[l][None], k_norm_g[l][None]
        q, qi, k, v, ki, wi = _proj_attn(h, gm, wl, attn_rows, gq, gk, tabs_seq)
        _, _, km, vm, kim, _ = _proj_attn(hm, gm, wl, attn_rows, gq, gk, tabs_meta)
        zg = _proj_gate(h, gm, wl, offs[6], offs[11] - offs[6])
        zg_meta = _proj_gate(hm, gm, wl, offs[6], 3 * CONV_WIDTH)

        pad_rows = lambda a: jnp.pad(a, ((0, MW - N_META), (0, 0)))
        r3 = lambda a: a.reshape(bsz, seq, a.shape[-1])
        later = [w_attn_branch[l], w_conv_branch[l], w_out[l],
                 ffn2_w_gate[l], ffn2_w_up[l], ffn2_w_down[l]]
        attn, wa, wc, wo, wg2, wu2, wd2 = _attention(
            r3(q), r3(qi), r3(wi), r3(k), r3(v), r3(ki),
            pad_rows(km), pad_rows(vm), pad_rows(kim), k_sel, later)
        attn = attn.reshape(bsz * seq, ATTN_WIDTH)

        h = _merge(h, attn, zg, zg_meta, conv_w[l], conv_b[l][None], wa, wc, wo, seq)

        h, = _ffn(h, ffn2_norm_g[l][None], wg2, wu2, wd2)

    return h.reshape(bsz, seq, d)
```

```python
import functools
import math

import jax
import jax.numpy as jnp
import numpy as np
from jax import lax
from jax.experimental import pallas as pl
from jax.experimental.pallas import tpu as pltpu

D_MODEL = 2048
N_META = 16
N_HEADS = 8
N_KV_HEADS = 2
HEAD_DIM = 128
ATTN_WIDTH = N_HEADS * HEAD_DIM
KV_WIDTH = N_KV_HEADS * HEAD_DIM
N_IDX_HEADS = 16
IDX_DIM = 64
IDX_WIDTH = N_IDX_HEADS * IDX_DIM
TOPK_MAX = 256
CONV_WIDTH = D_MODEL // 2
CONV_K = 3
D_FF = 5632
ROPE_THETA = 500000.0
ROT_DIV = 4
EPS = 1e-6

LANES = 128
VMEM_LIMIT = 56 * 1024 * 1024
NEG_BIG = -1e30
Q_SCALE = math.log2(math.e) / math.sqrt(HEAD_DIM)

FFN_TM, FFN_TF = 1024, 512
PROJ_TM = 256
GATE_TM, GATE_TN = 1024, 1792
MERGE_TM = 256
QB = 256
KC = 256
MW = 128
IDX_HPG = 1
BISECT_FIXED = 17
BISECT_STEPS = 3
HALO = 16


def _params(sem):
    return pltpu.CompilerParams(dimension_semantics=sem, vmem_limit_bytes=VMEM_LIMIT)


def _rms(x, g):
    ms = jnp.mean(x * x, axis=-1, keepdims=True)
    return x * lax.rsqrt(ms + EPS) * g


def _mm(a, b):
    return jnp.dot(a, b, preferred_element_type=jnp.float32)


def _ffn_kernel(emit_weights, h_ref, g_ref, wg_ref, wu_ref, wd_ref, o_ref, *rest):
    u_scr = rest[-1]
    j = pl.program_id(1)

    @pl.when(j == 0)
    def _():
        h = h_ref[...]
        u_scr[...] = _rms(h, g_ref[...]).astype(jnp.bfloat16)
        o_ref[...] = h

    bf = jnp.bfloat16
    wg, wu, wd = wg_ref[...].astype(bf), wu_ref[...].astype(bf), wd_ref[...].astype(bf)
    if emit_weights:
        for dst, w in zip(rest[:3], (wg, wu, wd)):
            dst[...] = w
    u = u_scr[...]
    half = FFN_TF // 2
    down = None
    for c in (slice(0, half), slice(half, FFN_TF)):
        gate = _mm(u, wg[:, c])
        up = _mm(u, wu[:, c])
        a = (gate * jax.nn.sigmoid(gate)) * (up * 0.5)
        part = _mm(a.astype(bf), wd[c, :])
        down = part if down is None else down + part
    o_ref[...] += down


def _ffn(h, g, wg, wu, wd, emit_weights=False):
    n = h.shape[0]
    tm = min(FFN_TM, n)
    assert not emit_weights or n == tm
    w_specs = [
        pl.BlockSpec((D_MODEL, FFN_TF), lambda i, j: (0, j)),
        pl.BlockSpec((D_MODEL, FFN_TF), lambda i, j: (0, j)),
        pl.BlockSpec((FFN_TF, D_MODEL), lambda i, j: (j, 0)),
    ]
    out_shape = [jax.ShapeDtypeStruct((n, D_MODEL), jnp.float32)]
    out_specs = [pl.BlockSpec((tm, D_MODEL), lambda i, j: (i, 0))]
    if emit_weights:
        out_shape += [jax.ShapeDtypeStruct(w.shape, jnp.bfloat16) for w in (wg, wu, wd)]
        out_specs += w_specs
    return pl.pallas_call(
        functools.partial(_ffn_kernel, emit_weights),
        out_shape=out_shape,
        grid=(n // tm, D_FF // FFN_TF),
        in_specs=[
            pl.BlockSpec((tm, D_MODEL), lambda i, j: (i, 0)),
            pl.BlockSpec((1, D_MODEL), lambda i, j: (0, 0)),
            *w_specs,
        ],
        out_specs=out_specs,
        scratch_shapes=[pltpu.VMEM((tm, D_MODEL), jnp.bfloat16)],
        compiler_params=_params(("parallel", "arbitrary")),
        name="ffn",
    )(h, g, wg, wu, wd)


def _rope(x, cos, s_lo, s_hi, half):
    n = x.shape[-1]
    return x * cos + pltpu.roll(x, n - half, 1) * s_lo + pltpu.roll(x, half, 1) * s_hi


def _proj_attn_kernel(h_ref, g_ref, wq_ref, wqi_ref, wk_ref, wv_ref, wkw_ref, gq_ref, gk_ref,
                      tab_ref, q_ref, qi_ref, k_ref, v_ref, ki_ref, wi_ref):
    bf = jnp.bfloat16
    u = _rms(h_ref[...], g_ref[...]).astype(bf)
    ca, sa_lo, sa_hi = tab_ref[0], tab_ref[1], tab_ref[2]
    ci, si_lo, si_hi = tab_ref[3], tab_ref[4], tab_ref[5]
    half_a = HEAD_DIM // ROT_DIV // 2
    half_i = IDX_DIM // ROT_DIV // 2
    gq = gq_ref[...]
    gk = gk_ref[...]
    z = _dot_nt(u, wq_ref[...])
    for hd in range(N_HEADS):
        sl = slice(hd * LANES, (hd + 1) * LANES)
        q = _rope(_rms(z[:, sl], gq), ca, sa_lo, sa_hi, half_a)
        q_ref[:, sl] = (q * Q_SCALE).astype(bf)
    z = _dot_nt(u, wqi_ref[...])
    for p in range(IDX_WIDTH // LANES):
        sl = slice(p * LANES, (p + 1) * LANES)
        qi_ref[:, sl] = _rope(z[:, sl], ci, si_lo, si_hi, half_i).astype(bf)
    z = _dot_nt(u, wk_ref[...])
    for hd in range(N_KV_HEADS):
        sl = slice(hd * LANES, (hd + 1) * LANES)
        k_ref[:, sl] = _rope(_rms(z[:, sl], gk), ca, sa_lo, sa_hi, half_a).astype(bf)
    v_ref[...] = _dot_nt(u, wv_ref[...]).astype(bf)
    x = _dot_nt(u, wkw_ref[...])
    lane = lax.broadcasted_iota(jnp.int32, x.shape, 1)
    is_ki = lane < IDX_DIM
    roped = _rope(x, jnp.where(is_ki, ci, 1.0), jnp.where(is_ki, si_lo, 0.0),
                  jnp.where(is_ki, si_hi, 0.0), half_i)
    swapped = pltpu.roll(roped, IDX_DIM, 1)
    ki_ref[...] = jnp.where(is_ki, roped, swapped).astype(bf)
    wi_ref[...] = swapped


def _proj_attn(h, g, wt, row_starts, gq, gk, tabs):
    n = h.shape[0]
    tm = min(PROJ_TM, n)
    nt = tabs.shape[1] // tm
    row = lambda i: (i, 0)
    const = lambda i: (0, 0)
    bf = jnp.bfloat16
    heights = (ATTN_WIDTH, IDX_WIDTH, KV_WIDTH, KV_WIDTH, LANES)
    assert all(r % 16 == 0 and r + hgt <= wt.shape[0] for r, hgt in zip(row_starts, heights))
    weight = lambda r0, hgt: pl.BlockSpec((pl.Element(hgt), pl.Element(D_MODEL)),
                                          lambda i: (r0, 0))
    return pl.pallas_call(
        _proj_attn_kernel,
        out_shape=(
            jax.ShapeDtypeStruct((n, ATTN_WIDTH), bf),
            jax.ShapeDtypeStruct((n, IDX_WIDTH), bf),
            jax.ShapeDtypeStruct((n, KV_WIDTH), bf),
            jax.ShapeDtypeStruct((n, KV_WIDTH), bf),
            jax.ShapeDtypeStruct((n, LANES), bf),
            jax.ShapeDtypeStruct((n, LANES), jnp.float32),
        ),
        grid=(n // tm,),
        in_specs=[
            pl.BlockSpec((tm, D_MODEL), row),
            pl.BlockSpec((1, D_MODEL), const),
            *[weight(r0, hgt) for r0, hgt in zip(row_starts, heights)],
            pl.BlockSpec((1, LANES), const),
            pl.BlockSpec((1, LANES), const),
            pl.BlockSpec((6, tm, LANES), lambda i: (0, i % nt, 0)),
        ],
        out_specs=(
            pl.BlockSpec((tm, ATTN_WIDTH), row),
            pl.BlockSpec((tm, IDX_WIDTH), row),
            pl.BlockSpec((tm, KV_WIDTH), row),
            pl.BlockSpec((tm, KV_WIDTH), row),
            pl.BlockSpec((tm, LANES), row),
            pl.BlockSpec((tm, LANES), row),
        ),
        compiler_params=_params(("parallel",)),
        name="proj_attn",
    )(h, g, wt, wt, wt, wt, wt, gq, gk, tabs)


def _proj_gate_kernel(h_ref, g_ref, w_ref, o_ref, u_scr):
    @pl.when(pl.program_id(1) == 0)
    def _():
        u_scr[...] = _rms(h_ref[...], g_ref[...]).astype(jnp.bfloat16)

    o_ref[...] = _dot_nt(u_scr[...], w_ref[...]).astype(o_ref.dtype)


def _proj_gate(h, g, wt, row0, cols):
    n = h.shape[0]
    tm = min(GATE_TM, n)
    cols = -(-cols // GATE_TN) * GATE_TN
    assert row0 + cols <= wt.shape[0] and row0 % 16 == 0
    return pl.pallas_call(
        _proj_gate_kernel,
        out_shape=jax.ShapeDtypeStruct((n, cols), jnp.bfloat16),
        grid=(n // tm, cols // GATE_TN),
        in_specs=[
            pl.BlockSpec((tm, D_MODEL), lambda i, j: (i, 0)),
            pl.BlockSpec((1, D_MODEL), lambda i, j: (0, 0)),
            pl.BlockSpec((pl.Element(GATE_TN), pl.Element(D_MODEL)),
                         lambda i, j: (pl.multiple_of(row0 + j * GATE_TN, 16), 0)),
        ],
        out_specs=pl.BlockSpec((tm, GATE_TN), lambda i, j: (i, j)),
        scratch_shapes=[pltpu.VMEM((tm, D_MODEL), jnp.bfloat16)],
        compiler_params=_params(("parallel", "arbitrary")),
        name="proj_gate",
    )(h, g, wt)


def _dot_nt(a, b):
    return lax.dot_general(a, b, (((1,), (1,)), ((), ())), preferred_element_type=jnp.float32)


def _attn_kernel(k_sel, n_cast, q_ref, qi_ref, wi_ref, k_ref, v_ref, ki_ref, km_ref, vm_ref,
                 kim_ref, *rest):
    cast_in, o_ref, cast_out = rest[:n_cast], rest[n_cast], rest[n_cast + 1:2 * n_cast + 1]
    (qs_scr, qis_scr, sctm_scr, sct_scr, biasm_scr, bias_scr,
     m_scr, l_scr, acc_scr) = rest[2 * n_cast + 1:]
    for src, dst in zip(cast_in, cast_out):
        dst[...] = src[...].astype(dst.dtype)

    qb = pl.program_id(1)
    n_seq = qb + 1
    rep = N_HEADS // N_KV_HEADS
    hpg = IDX_HPG
    lane_q = lax.broadcasted_iota(jnp.int32, (QB, LANES), 1)
    neg_inf = float("-inf")

    for g in range(N_KV_HEADS):
        for r in range(rep):
            hd = g * rep + r
            qs_scr[g, r * QB:(r + 1) * QB, :] = q_ref[0, :, hd * LANES:(hd + 1) * LANES]
    for i in range(N_IDX_HEADS // hpg):
        for r in range(hpg):
            hd = i * hpg + r
            pair = qi_ref[0, :, (hd // 2) * LANES:(hd // 2 + 1) * LANES]
            keep = (lane_q < IDX_DIM) if hd % 2 == 0 else (lane_q >= IDX_DIM)
            qis_scr[i, r * QB:(r + 1) * QB, :] = jnp.where(keep, pair, jnp.zeros_like(pair))

    w_all = wi_ref[0]

    def index_scores(ki_c):
        kw = ki_c.shape[0]
        score = jnp.zeros((QB, kw), jnp.float32)
        for i in range(N_IDX_HEADS // hpg):
            d = _dot_nt(qis_scr[i], ki_c)
            for r in range(hpg):
                hd = i * hpg + r
                score = score + jnp.maximum(d[r * QB:(r + 1) * QB], 0.0) * w_all[:, hd:hd + 1]
        return score

    lane_m = lax.broadcasted_iota(jnp.int32, (QB, MW), 1)
    sctm_scr[...] = jnp.where(lane_m < N_META, index_scores(kim_ref[...]), neg_inf).T
    row = lax.broadcasted_iota(jnp.int32, (QB, KC), 0)
    lane = lax.broadcasted_iota(jnp.int32, (QB, KC), 1)

    def score_chunk(c):
        ki_c = ki_ref[0, pl.ds(pl.multiple_of(c * KC, KC), KC), :]
        causal = (c - qb) * KC + lane <= row
        s = jnp.where(causal, index_scores(ki_c), neg_inf)
        bias_scr[c] = s
        sct_scr[c] = s.T

    def score_pair(i, carry):
        score_chunk(2 * i)
        score_chunk(2 * i + 1)
        return carry

    lax.fori_loop(0, n_seq // 2, score_pair, 0)

    @pl.when(n_seq % 2 == 1)
    def _():
        score_chunk(n_seq - 1)
        sct_scr[n_seq] = jnp.full((KC, QB), neg_inf, jnp.float32)

    def over_keys(f, init):
        def pair(i, a):
            c = 2 * i
            a = f(sct_scr[c], MW + c * KC, a)
            return f(sct_scr[c + 1], MW + (c + 1) * KC, a)
        return lax.fori_loop(0, (n_seq + 1) // 2, pair, f(sctm_scr[...], 0, init))

    def fold8(x):
        return jnp.sum(x.reshape(x.shape[0] // 8, 8, QB), axis=0)

    def count(pred):
        part = over_keys(lambda x, p0, a: a + fold8(jnp.where(pred(x, p0), 1.0, 0.0)),
                         jnp.zeros((8, QB), jnp.float32))
        return jnp.sum(part, axis=0, keepdims=True)

    kf = float(k_sel)
    qlane = lax.broadcasted_iota(jnp.int32, (1, QB), 1)
    n_valid = (N_META + 1 + qb * QB + qlane).astype(jnp.float32)

    def minmax(x, p0, carry):
        mn, mx = carry
        x3 = x.reshape(x.shape[0] // 8, 8, QB)
        mx = jnp.maximum(mx, jnp.max(x3, axis=0))
        mn = jnp.minimum(mn, jnp.min(jnp.where(x3 == neg_inf, float("inf"), x3), axis=0))
        return mn, mx

    mn, mx = over_keys(minmax, (jnp.full((8, QB), float("inf"), jnp.float32),
                                jnp.full((8, QB), neg_inf, jnp.float32)))
    mn = jnp.min(mn, axis=0, keepdims=True)
    mx = jnp.max(mx, axis=0, keepdims=True)

    def count_ge(t):
        return count(lambda x, p0: x >= t)

    def midpoint(lo, hi):
        return lo + (hi - lo) * 0.5

    def is_active(lo, hi, cnt_lo):
        mid = midpoint(lo, hi)
        return jnp.where((cnt_lo != kf) & (mid > lo) & (mid < hi), 1.0, 0.0)

    cnt_mx = count_ge(mx)
    at_max = cnt_mx >= kf
    lo0 = jnp.where(at_max, mx, mn)
    cnt0 = jnp.where(at_max, cnt_mx, n_valid)
    act0 = jnp.where(n_valid > kf, is_active(lo0, mx, cnt0), 0.0)

    def bisect_once(state):
        lo, hi, cnt_lo, act = state
        mid = midpoint(lo, hi)
        cnt = count_ge(mid)
        up = (act > 0.5) & (cnt >= kf)
        down = (act > 0.5) & (cnt < kf)
        lo = jnp.where(up, mid, lo)
        cnt_lo = jnp.where(up, cnt, cnt_lo)
        hi = jnp.where(down, mid, hi)
        return lo, hi, cnt_lo, act * is_active(lo, hi, cnt_lo)

    def bisect_trip(st):
        for _ in range(BISECT_STEPS):
            st = bisect_once(st)
        return st

    st = lax.fori_loop(0, BISECT_FIXED, lambda _, st: bisect_once(st), (lo0, mx, cnt0, act0))
    thr, _, cnt_thr, _ = lax.while_loop(lambda st: jnp.max(st[3]) > 0.5, bisect_trip, st)
    has_tie = cnt_thr > kf
    any_tie = jnp.max(jnp.where(has_tie, 1.0, 0.0)) > 0.5

    def write_bias(keep):
        biasm_scr[...] = jnp.where(keep(sctm_scr[...], 0), 0.0, NEG_BIG).T

        def body(c, carry):
            bias_scr[c] = jnp.where(keep(sct_scr[c], MW + c * KC), 0.0, NEG_BIG).T
            return carry
        lax.fori_loop(0, n_seq, body, 0)

    @pl.when(jnp.logical_not(any_tie))
    def _():
        biasm_scr[...] = jnp.where(sctm_scr[...] >= thr, 0.0, NEG_BIG).T
        thr_q = jnp.broadcast_to(thr, (LANES, QB)).T
        thr_q = jnp.concatenate([thr_q] * (KC // LANES), axis=1)

        def body(c, carry):
            bias_scr[c] = jnp.where(bias_scr[c] >= thr_q, 0.0, NEG_BIG)
            return carry
        lax.fori_loop(0, n_seq, body, 0)

    @pl.when(any_tie)
    def _():
        pos_bits = (MW + k_ref.shape[1] - 1).bit_length()
        far = 1 << pos_bits

        def tie_pos(x, p0):
            pos = p0 + lax.broadcasted_iota(jnp.int32, x.shape, 0)
            return jnp.where(x == thr, pos, far)

        need = kf - count(lambda x, p0: x > thr)

        def pos_step(i, last):
            cand = last | lax.shift_left(jnp.int32(1), pos_bits - 1 - i)
            below = count(lambda x, p0: tie_pos(x, p0) < cand)
            return jnp.where(below < need, cand, last)

        last = lax.fori_loop(0, pos_bits, pos_step, jnp.zeros((1, QB), jnp.int32))
        last = jnp.where(has_tie, last, far - 1)
        write_bias(lambda x, p0: jnp.where(x > thr, 1.0,
                                           jnp.where(tie_pos(x, p0) <= last, 1.0, 0.0)) > 0.5)

    m_scr[...] = jnp.full(m_scr.shape, NEG_BIG, jnp.float32)
    l_scr[...] = jnp.zeros(l_scr.shape, jnp.float32)
    acc_scr[...] = jnp.zeros(acc_scr.shape, jnp.float32)

    def attend(bias, k_c, v_c):
        kw = k_c.shape[0]
        for g in range(N_KV_HEADS):
            sl = slice(g * LANES, (g + 1) * LANES)
            s = _dot_nt(qs_scr[g], k_c[:, sl])
            s = jnp.concatenate([s[r * QB:(r + 1) * QB] + bias for r in range(rep)], axis=0)
            m_prev = m_scr[g]
            m_new = jnp.maximum(m_prev, jnp.max(s, axis=-1, keepdims=True))
            alpha = jnp.exp2(m_prev - m_new)
            p = jnp.exp2(s - jnp.concatenate([m_new] * (kw // LANES), axis=1))
            l_scr[g] = alpha * l_scr[g] + jnp.sum(p, axis=-1, keepdims=True)
            acc_scr[g] = alpha * acc_scr[g] + _mm(p.astype(jnp.bfloat16), v_c[:, sl])
            m_scr[g] = m_new

    attend(biasm_scr[...], km_ref[...], vm_ref[...])

    def attend_pair(i, carry):
        c = 2 * i
        rows = pl.ds(pl.multiple_of(c * KC, 2 * KC), 2 * KC)
        attend(jnp.concatenate([bias_scr[c], bias_scr[c + 1]], axis=1),
               k_ref[0, rows, :], v_ref[0, rows, :])
        return carry

    lax.fori_loop(0, n_seq // 2, attend_pair, 0)

    @pl.when(n_seq % 2 == 1)
    def _():
        c = n_seq - 1
        rows = pl.ds(pl.multiple_of(c * KC, KC), KC)
        attend(bias_scr[c], k_ref[0, rows, :], v_ref[0, rows, :])

    for g in range(N_KV_HEADS):
        o = acc_scr[g] / l_scr[g]
        for r in range(rep):
            hd = g * rep + r
            o_ref[0, :, hd * LANES:(hd + 1) * LANES] = o[r * QB:(r + 1) * QB].astype(o_ref.dtype)


def _cast_spec(w, nb, nq):
    rows, cols = w.shape
    if rows % (nb * nq * 16) == 0:
        tr = rows // (nb * nq)
        return pl.BlockSpec((tr, cols), lambda bi, qi_: (bi * nq + qi_, 0))
    assert rows % (nb * 16) == 0 and cols % (nq * LANES) == 0, w.shape
    return pl.BlockSpec((rows // nb, cols // nq), lambda bi, qi_: (bi, qi_))


def _attention(q, qi, wi, k, v, ki, km, vm, kim, k_sel, cast_weights):
    b, s, _ = q.shape
    nq = s // QB
    blk = lambda bi, qi_: (bi, qi_, 0)
    full = lambda bi, qi_: (bi, 0, 0)
    const = lambda bi, qi_: (0, 0)
    rep = N_HEADS // N_KV_HEADS
    f32 = jnp.float32
    cast_specs = [_cast_spec(w, b, nq) for w in cast_weights]
    return pl.pallas_call(
        functools.partial(_attn_kernel, k_sel, len(cast_weights)),
        out_shape=(jax.ShapeDtypeStruct((b, s, ATTN_WIDTH), jnp.bfloat16),
                   *[jax.ShapeDtypeStruct(w.shape, jnp.bfloat16) for w in cast_weights]),
        grid=(b, nq),
        in_specs=[
            pl.BlockSpec((1, QB, ATTN_WIDTH), blk),
            pl.BlockSpec((1, QB, IDX_WIDTH), blk),
            pl.BlockSpec((1, QB, LANES), blk),
            pl.BlockSpec((1, s, KV_WIDTH), full),
            pl.BlockSpec((1, s, KV_WIDTH), full),
            pl.BlockSpec((1, s, LANES), full),
            pl.BlockSpec((MW, KV_WIDTH), const),
            pl.BlockSpec((MW, KV_WIDTH), const),
            pl.BlockSpec((MW, LANES), const),
            *cast_specs,
        ],
        out_specs=(pl.BlockSpec((1, QB, ATTN_WIDTH), blk), *cast_specs),
        scratch_shapes=[
            pltpu.VMEM((N_KV_HEADS, rep * QB, LANES), jnp.bfloat16),
            pltpu.VMEM((N_IDX_HEADS // IDX_HPG, IDX_HPG * QB, LANES), jnp.bfloat16),
            pltpu.VMEM((MW, QB), f32),
            pltpu.VMEM((s // KC, KC, QB), f32),
            pltpu.VMEM((QB, MW), f32),
            pltpu.VMEM((s // KC, QB, KC), f32),
            pltpu.VMEM((N_KV_HEADS, rep * QB, LANES), f32),
            pltpu.VMEM((N_KV_HEADS, rep * QB, LANES), f32),
            pltpu.VMEM((N_KV_HEADS, rep * QB, LANES), f32),
        ],
        compiler_params=_params(("parallel", "arbitrary")),
        name="dsa_attention",
    )(q, qi, wi, k, v, ki, km, vm, kim, *cast_weights)


def _merge_kernel(tiles_per_seq, h_ref, attn_ref, xc_ref, gb_ref, cg_ref, ga0_ref, ga1_ref,
                  gc0_ref, gc1_ref, xc_prev_ref, cg_prev_ref, xc_meta_ref, cg_meta_ref,
                  cw_ref, cb_ref, wa_ref, wc_ref, wo_ref, o_ref):
    f32 = jnp.float32
    first = pl.program_id(0) % tiles_per_seq == 0
    gx = cg_ref[...].astype(f32) * xc_ref[...].astype(f32)
    halo_prev = cg_prev_ref[...].astype(f32) * xc_prev_ref[...].astype(f32)
    halo_meta = cg_meta_ref[...].astype(f32) * xc_meta_ref[...].astype(f32)
    halo = jnp.where(first, halo_meta, halo_prev)
    h1, h2 = halo[HALO - 1:HALO], halo[HALO - 2:HALO - 1]
    row = lax.broadcasted_iota(jnp.int32, gx.shape, 0)
    prev1 = jnp.where(row == 0, h1, pltpu.roll(gx, 1, 0))
    prev2 = jnp.where(row == 0, h2, jnp.where(row == 1, h1, pltpu.roll(gx, 2, 0)))
    cw = cw_ref[...]
    conv = cw[0:1] * prev2 + cw[1:2] * prev1 + cw[2:3] * gx + cb_ref[...]
    feat = (gb_ref[...].astype(f32) * conv).astype(jnp.bfloat16)
    y_conv = _mm(feat, wc_ref[...])
    y_attn = _mm(attn_ref[...], wa_ref[...])
    half = D_MODEL // 2
    sig = lambda r: jax.nn.sigmoid(r[...].astype(f32))
    merged = jnp.concatenate(
        [sig(ga0_ref) * y_attn[:, :half] + sig(gc0_ref) * y_conv[:, :half],
         sig(ga1_ref) * y_attn[:, half:] + sig(gc1_ref) * y_conv[:, half:]], axis=1)
    o_ref[...] = h_ref[...] + _mm(merged.astype(jnp.bfloat16), wo_ref[...])


def _merge(h, attn, zg, zg_meta, conv_w, conv_b, wa, wc, wo, seq):
    n = h.shape[0]
    tm = MERGE_TM
    tiles_per_seq = seq // tm
    row2 = lambda i: (i, 0)
    const = lambda i: (0, 0)
    col = lambda c: pl.BlockSpec((tm, CONV_WIDTH), lambda i: (i, c))
    prev = lambda c: pl.BlockSpec((HALO, CONV_WIDTH),
                                  lambda i: (jnp.maximum(i * (tm // HALO) - 1, 0), c))
    meta = lambda c: pl.BlockSpec((HALO, CONV_WIDTH), lambda i: (N_META // HALO - 1, c))
    single = pl.Buffered(1)
    return pl.pallas_call(
        functools.partial(_merge_kernel, tiles_per_seq),
        out_shape=jax.ShapeDtypeStruct((n, D_MODEL), jnp.float32),
        grid=(n // tm,),
        in_specs=[
            pl.BlockSpec((tm, D_MODEL), row2),
            pl.BlockSpec((tm, ATTN_WIDTH), row2),
            col(0), col(1), col(2), col(3), col(4), col(5), col(6),
            prev(0), prev(2), meta(0), meta(2),
            pl.BlockSpec((CONV_K, CONV_WIDTH), const),
            pl.BlockSpec((1, CONV_WIDTH), const),
            pl.BlockSpec((ATTN_WIDTH, D_MODEL), const, pipeline_mode=single),
            pl.BlockSpec((CONV_WIDTH, D_MODEL), const, pipeline_mode=single),
            pl.BlockSpec((D_MODEL, D_MODEL), const, pipeline_mode=single),
        ],
        out_specs=pl.BlockSpec((tm, D_MODEL), row2),
        compiler_params=_params(("parallel",)),
        name="merge",
    )(h, attn, zg, zg, zg, zg, zg, zg, zg, zg, zg, zg_meta, zg_meta, conv_w, conv_b, wa, wc, wo)


def _rope_tables(pos, head_dim):
    rot = head_dim // ROT_DIV
    half = rot // 2
    inv = ROPE_THETA ** (-jnp.arange(0, rot, 2, dtype=jnp.float32) / rot)
    ang = pos.astype(jnp.float32)[:, None] * inv[None, :]
    cos, sin = jnp.cos(ang), jnp.sin(ang)
    n = pos.shape[0]
    pad = head_dim - rot
    c = jnp.concatenate([cos, cos, jnp.ones((n, pad), jnp.float32)], axis=1)
    zeros_h = jnp.zeros((n, half), jnp.float32)
    zeros_p = jnp.zeros((n, pad), jnp.float32)
    s_lo = jnp.concatenate([-sin, zeros_h, zeros_p], axis=1)
    s_hi = jnp.concatenate([zeros_h, sin, zeros_p], axis=1)
    tile = LANES // head_dim
    return [jnp.tile(t, (1, tile)) for t in (c, s_lo, s_hi)]


def kernel(x, meta_tokens, ffn1_norm_g, ffn1_w_gate, ffn1_w_up, ffn1_w_down, mix_norm_g, w_in, q_norm_g, k_norm_g, conv_w, conv_b, w_attn_branch, w_conv_branch, w_out, ffn2_norm_g, ffn2_w_gate, ffn2_w_up, ffn2_w_down):
    bsz, seq, d = x.shape
    bf = jnp.bfloat16
    k_sel = min(TOPK_MAX, seq // 4)
    assert d == D_MODEL and seq % QB == 0 and k_sel <= KC

    widths = [ATTN_WIDTH, KV_WIDTH, KV_WIDTH, IDX_WIDTH, IDX_DIM, N_IDX_HEADS,
              CONV_WIDTH, CONV_WIDTH, CONV_WIDTH, D_MODEL, D_MODEL]
    offs = [int(o) for o in np.concatenate([[0], np.cumsum(widths)])]
    attn_rows = (offs[0], offs[3], offs[1], offs[2], offs[4])

    h = x.reshape(bsz * seq, d)
    hm = meta_tokens.astype(x.dtype)
    pos_seq = jnp.arange(N_META, N_META + seq, dtype=jnp.int32)
    pos_meta = jnp.arange(N_META, dtype=jnp.int32)
    tabs_seq = jnp.stack(_rope_tables(pos_seq, HEAD_DIM) + _rope_tables(pos_seq, IDX_DIM))
    tabs_meta = jnp.stack(_rope_tables(pos_meta, HEAD_DIM) + _rope_tables(pos_meta, IDX_DIM))

    for l in range(ffn1_norm_g.shape[0]):
        wl = w_in[l].T.astype(bf)
        g1 = ffn1_norm_g[l][None]
        gm = mix_norm_g[l][None]

        hm, wg1, wu1, wd1 = _ffn(hm, g1, ffn1_w_gate[l], ffn1_w_up[l], ffn1_w_down[l],
                                 emit_weights=True)
        h, = _ffn(h, g1, wg1, wu1, wd1)

        gq, gk = q_norm_g[l][None], k_norm_g[l][None]
        q, qi, k, v, ki, wi = _proj_attn(h, gm, wl, attn_rows, gq, gk, tabs_seq)
        _, _, km, vm, kim, _ = _proj_attn(hm, gm, wl, attn_rows, gq, gk, tabs_meta)
        zg = _proj_gate(h, gm, wl, offs[6], offs[11] - offs[6])
        zg_meta = _proj_gate(hm, gm, wl, offs[6], 3 * CONV_WIDTH)

        pad_rows = lambda a: jnp.pad(a, ((0, MW - N_META), (0, 0)))
        r3 = lambda a: a.reshape(bsz, seq, a.shape[-1])
        later = [w_attn_branch[l], w_conv_branch[l], w_out[l],
                 ffn2_w_gate[l], ffn2_w_up[l], ffn2_w_down[l]]
        attn, wa, wc, wo, wg2, wu2, wd2 = _attention(
            r3(q), r3(qi), r3(wi), r3(k), r3(v), r3(ki),
            pad_rows(km), pad_rows(vm), pad_rows(kim), k_sel, later)
        attn = attn.reshape(bsz * seq, ATTN_WIDTH)

        h = _merge(h, attn, zg, zg_meta, conv_w[l], conv_b[l][None], wa, wc, wo, seq)

        h, = _ffn(h, ffn2_norm_g[l][None], wg2, wu2, wd2)

    return h.reshape(bsz, seq, d)
```
